```python
import jax
import jax.numpy as jnp
from jax import lax
import numpy as np

D_MODEL = 1024
BATCH = 8
SEQ = 2048
DEPTH = 4
DEC_BATCH = 128
DEC_SEQ = 8
PAST_LEN = 16384
PAGE_SIZE = 128

N_META = 16
D_POOL = D_MODEL
POOL_WINDOWS = (2, 4, 8, 16)
N_POOL_GROUPS = len(POOL_WINDOWS)
POOL_GROUP_DIM = D_POOL // N_POOL_GROUPS
POOL_BUF = max(POOL_WINDOWS) - 1
D_INNER = 2 * D_MODEL
SSD_HEAD_DIM = 64
SSD_HEADS = D_INNER // SSD_HEAD_DIM
SSD_GROUPS = 4
HEADS_PER_GROUP = SSD_HEADS // SSD_GROUPS
D_STATE = 128
CONV_WIDTH = 4
CONV_DIM = D_INNER + 2 * SSD_GROUPS * D_STATE
CHUNK = 128
D_FF = 2816
FFN_CONV_WIDTH = 3
EPS = 1e-6
OFF_U = 0
OFF_Z = OFF_U + D_POOL
OFF_XBC = OFF_Z + D_INNER
OFF_DT = OFF_XBC + CONV_DIM
OFF_GA = OFF_DT + SSD_HEADS
OFF_GB = OFF_GA + D_MODEL
D_IN_TOTAL = OFF_GB + D_MODEL

kernel_name = 'hybrid_pool_ssd_convffn_step'


def rmsnorm(x, w):
    xf = x.astype(jnp.float32)
    y = xf * lax.rsqrt(jnp.mean(xf * xf, axis=-1, keepdims=True) + EPS)
    return (y * w.astype(jnp.float32)).astype(x.dtype)


def causal_dwconv(u, prefix, w, b):
    width = w.shape[0]
    L = u.shape[1]
    ext = jnp.concatenate([prefix.astype(u.dtype), u], axis=1)
    out = b
    for k in range(width):
        out = out + ext[:, k:k + L] * w[k]
    return out, ext[:, -(width - 1):]


def pool_mix(u, prefix, start_pos, pool_w, pool_scale):
    b, L, _ = u.shape
    ext = jnp.concatenate([prefix.astype(u.dtype), u], axis=1)
    cs = jnp.concatenate([jnp.zeros((b, 1, D_POOL), jnp.float32),
                          jnp.cumsum(ext.astype(jnp.float32), axis=1)], axis=1)
    pos = start_pos + jnp.arange(L, dtype=jnp.int32)
    uf = u.astype(jnp.float32)
    outs = []
    for g, win in enumerate(POOL_WINDOWS):
        c0, c1 = g * POOL_GROUP_DIM, (g + 1) * POOL_GROUP_DIM
        hi = cs[:, POOL_BUF + 1:POOL_BUF + 1 + L, c0:c1]
        lo = cs[:, POOL_BUF + 1 - win:POOL_BUF + 1 - win + L, c0:c1]
        count = jnp.minimum(pos + 1, win).astype(jnp.float32)[None, :, None]
        outs.append((hi - lo) / count - uf[:, :, c0:c1])
    d = jnp.stack(outs, axis=2).astype(u.dtype)
    mixed = jnp.einsum('blgc,gcd->blgd', d, pool_w).reshape(b, L, D_POOL) * pool_scale
    return mixed, ext[:, -POOL_BUF:]


def ssd_scan(xh, dt, a, bm, cm, h0):
    b, L = xh.shape[0], xh.shape[1]
    q = CHUNK if L % CHUNK == 0 else L
    nc = L // q

    def chunks(t):
        return jnp.moveaxis(t.reshape((b, nc, q) + t.shape[2:]), 1, 0)

    x_c = chunks(xh.astype(jnp.float32).reshape(b, L, SSD_GROUPS, HEADS_PER_GROUP, SSD_HEAD_DIM))
    dt_c = chunks(dt.reshape(b, L, SSD_GROUPS, HEADS_PER_GROUP))
    b_c = chunks(bm.astype(jnp.float32))
    c_c = chunks(cm.astype(jnp.float32))
    a_g = a.reshape(SSD_GROUPS, HEADS_PER_GROUP)
    causal = jnp.tril(jnp.ones((q, q), dtype=bool))[None, :, :, None, None]

    def step(h, inp):
        x, d, bc, cc = inp
        cum = jnp.cumsum(d * a_g, axis=1)
        seg = cum[:, :, None] - cum[:, None]
        decay = jnp.where(causal, jnp.exp(jnp.where(causal, seg, 0.0)), 0.0)
        xdt = x * d[..., None]
        scores = jnp.einsum('btgn,bsgn->btsg', cc, bc)
        m = scores[..., None] * decay
        y = jnp.einsum('btsgh,bsghp->btghp', m, xdt)
        y = y + jnp.einsum('btgn,bghpn->btghp', cc, h) * jnp.exp(cum)[..., None]
        last = cum[:, -1]
        xw = xdt * jnp.exp(last[:, None] - cum)[..., None]
        h = h * jnp.exp(last)[..., None, None] + jnp.einsum('bsgn,bsghp->bghpn', bc, xw)
        return h, y

    h_init = h0.astype(jnp.float32).reshape(b, SSD_GROUPS, HEADS_PER_GROUP, SSD_HEAD_DIM, D_STATE)
    h, y = lax.scan(step, h_init, (x_c, dt_c, b_c, c_c))
    y = jnp.moveaxis(y, 0, 1).reshape(b, L, SSD_HEADS, SSD_HEAD_DIM)
    return y, h.reshape(b, SSD_HEADS, SSD_HEAD_DIM, D_STATE)


def ssd(xh, dt, a, bm, cm, h0, n_lead):
    if n_lead > 0:
        y0, h = ssd_scan(xh[:, :n_lead], dt[:, :n_lead], a, bm[:, :n_lead], cm[:, :n_lead], h0)
        y1, h = ssd_scan(xh[:, n_lead:], dt[:, n_lead:], a, bm[:, n_lead:], cm[:, n_lead:], h)
        return jnp.concatenate([y0, y1], axis=1), h
    return ssd_scan(xh, dt, a, bm, cm, h0)


def gated_rmsnorm(y, z, w):
    yf = y.astype(jnp.float32) * jax.nn.silu(z.astype(jnp.float32))
    yg = yf.reshape(yf.shape[:-1] + (SSD_GROUPS, D_INNER // SSD_GROUPS))
    yg = yg * lax.rsqrt(jnp.mean(yg * yg, axis=-1, keepdims=True) + EPS)
    return (yg.reshape(yf.shape) * w.astype(jnp.float32)).astype(z.dtype)


def hybrid_layer(x, n_lead, start_pos, pool_buf, conv_buf, ssm_h0, ffn_buf,
                 norm1_w, w_in, pool_w, pool_scale, w_pool_out, conv_w, conv_b,
                 dt_bias, a_log, d_skip, ssd_norm_w, w_ssd_out, w_o,
                 norm2_w, w_up, ffn_conv_w, ffn_conv_b, w_down):
    b, L, _ = x.shape
    hn = rmsnorm(x, norm1_w)
    proj = hn @ w_in
    u = proj[..., OFF_U:OFF_Z]
    z = proj[..., OFF_Z:OFF_XBC]
    xbc = proj[..., OFF_XBC:OFF_DT]
    dt_raw = proj[..., OFF_DT:OFF_GA]
    g_a = proj[..., OFF_GA:OFF_GB]
    g_b = proj[..., OFF_GB:D_IN_TOTAL]
    pooled, pool_new = pool_mix(u, pool_buf, start_pos, pool_w, pool_scale)
    a_out = pooled @ w_pool_out
    xbc_c, conv_new = causal_dwconv(xbc, conv_buf, conv_w, conv_b)
    xbc_c = jax.nn.silu(xbc_c)
    xs = xbc_c[..., :D_INNER]
    bm = xbc_c[..., D_INNER:D_INNER + SSD_GROUPS * D_STATE].reshape(b, L, SSD_GROUPS, D_STATE)
    cm = xbc_c[..., D_INNER + SSD_GROUPS * D_STATE:].reshape(b, L, SSD_GROUPS, D_STATE)
    dt = jax.nn.softplus(dt_raw.astype(jnp.float32) + dt_bias.astype(jnp.float32))
    a = -jnp.exp(a_log.astype(jnp.float32))
    xh = xs.reshape(b, L, SSD_HEADS, SSD_HEAD_DIM)
    y, h_new = ssd(xh, dt, a, bm, cm, ssm_h0, n_lead)
    y = y + d_skip.astype(jnp.float32)[:, None] * xh.astype(jnp.float32)
    y = gated_rmsnorm(y.reshape(b, L, D_INNER), z, ssd_norm_w).astype(x.dtype)
    b_out = y @ w_ssd_out
    merged = jax.nn.sigmoid(g_a) * a_out + jax.nn.sigmoid(g_b) * b_out
    x = x + merged @ w_o
    hn2 = rmsnorm(x, norm2_w)
    up, ffn_new = causal_dwconv(hn2 @ w_up, ffn_buf, ffn_conv_w, ffn_conv_b)
    x = x + (jax.nn.silu(up[..., :D_FF]) * up[..., D_FF:]) @ w_down
    return x, pool_new, conv_new, h_new, ffn_new


def setup_inputs(seed: int = 0) -> dict:
    key = jax.random.key(seed)
    ks = jax.random.split(key, 32)
    f32 = jnp.float32

    def nrm(k, shape, scale):
        return jax.random.normal(k, shape, f32) * scale

    dt_init = jnp.exp(jax.random.uniform(ks[12], (DEPTH, SSD_HEADS), f32, np.log(1e-3), np.log(1e-1)))
    dt_bias = dt_init + jnp.log(-jnp.expm1(-dt_init))
    return {
        'x_prompt': nrm(ks[0], (BATCH, SEQ, D_MODEL), 1.0),
        'x_sample': nrm(ks[1], (DEC_BATCH, DEC_SEQ, D_MODEL), 1.0),
        'state_pool': nrm(ks[2], (DEPTH, DEC_BATCH, POOL_BUF, D_POOL), 1.0),
        'state_conv': nrm(ks[3], (DEPTH, DEC_BATCH, CONV_WIDTH - 1, CONV_DIM), 1.0),
        'state_ssm': nrm(ks[4], (DEPTH, DEC_BATCH, SSD_HEADS, SSD_HEAD_DIM, D_STATE), 0.1),
        'state_ffn': nrm(ks[5], (DEPTH, DEC_BATCH, FFN_CONV_WIDTH - 1, 2 * D_FF), 1.0),
        'meta_tokens': nrm(ks[6], (N_META, D_MODEL), 1.0),
        'norm1_w': 1.0 + nrm(ks[7], (DEPTH, D_MODEL), 0.01),
        'w_in': nrm(ks[8], (DEPTH, D_MODEL, D_IN_TOTAL), D_MODEL ** -0.5),
        'pool_w': nrm(ks[9], (DEPTH, N_POOL_GROUPS, POOL_GROUP_DIM, POOL_GROUP_DIM), POOL_GROUP_DIM ** -0.5),
        'pool_scale': 1.0 + nrm(ks[10], (DEPTH, D_POOL), 0.01),
        'w_pool_out': nrm(ks[11], (DEPTH, D_POOL, D_MODEL), D_POOL ** -0.5),
        'conv_w': nrm(ks[13], (DEPTH, CONV_WIDTH, CONV_DIM), CONV_WIDTH ** -0.5),
        'conv_b': nrm(ks[14], (DEPTH, CONV_DIM), 0.01),
        'dt_bias': dt_bias,
        'a_log': jnp.log(jax.random.uniform(ks[15], (DEPTH, SSD_HEADS), f32, 1.0, 16.0)),
        'd_skip': 1.0 + nrm(ks[16], (DEPTH, SSD_HEADS), 0.1),
        'ssd_norm_w': 1.0 + nrm(ks[17], (DEPTH, D_INNER), 0.01),
        'w_ssd_out': nrm(ks[18], (DEPTH, D_INNER, D_MODEL), D_INNER ** -0.5),
        'w_o': nrm(ks[19], (DEPTH, D_MODEL, D_MODEL), D_MODEL ** -0.5),
        'norm2_w': 1.0 + nrm(ks[20], (DEPTH, D_MODEL), 0.01),
        'w_up': nrm(ks[21], (DEPTH, D_MODEL, 2 * D_FF), D_MODEL ** -0.5),
        'ffn_conv_w': nrm(ks[22], (DEPTH, FFN_CONV_WIDTH, 2 * D_FF), FFN_CONV_WIDTH ** -0.5),
        'ffn_conv_b': nrm(ks[23], (DEPTH, 2 * D_FF), 0.01),
        'w_down': nrm(ks[24], (DEPTH, D_FF, D_MODEL), D_FF ** -0.5),
        'final_norm_w': 1.0 + nrm(ks[25], (D_MODEL,), 0.01),
    }


def reference(x_prompt, x_sample, state_pool, state_conv, state_ssm, state_ffn,
              meta_tokens, norm1_w, w_in, pool_w, pool_scale, w_pool_out, conv_w, conv_b,
              dt_bias, a_log, d_skip, ssd_norm_w, w_ssd_out, w_o, norm2_w, w_up,
              ffn_conv_w, ffn_conv_b, w_down, final_norm_w):
    b_p = x_prompt.shape[0]
    dt_ = x_prompt.dtype
    meta = jnp.broadcast_to(meta_tokens.astype(dt_)[None], (b_p, N_META, D_MODEL))
    xp = jnp.concatenate([meta, x_prompt], axis=1)
    xs = x_sample
    pp, pc, ph, pf = [], [], [], []
    sp, sc, sh, sf = [], [], [], []
    for l in range(DEPTH):
        params = (norm1_w[l], w_in[l], pool_w[l], pool_scale[l], w_pool_out[l], conv_w[l], conv_b[l],
                  dt_bias[l], a_log[l], d_skip[l], ssd_norm_w[l], w_ssd_out[l], w_o[l],
                  norm2_w[l], w_up[l], ffn_conv_w[l], ffn_conv_b[l], w_down[l])
        xp, p_pool, p_conv, p_h, p_ffn = hybrid_layer(
            xp, N_META, 0,
            jnp.zeros((b_p, POOL_BUF, D_POOL), dt_),
            jnp.zeros((b_p, CONV_WIDTH - 1, CONV_DIM), dt_),
            jnp.zeros((b_p, SSD_HEADS, SSD_HEAD_DIM, D_STATE), jnp.float32),
            jnp.zeros((b_p, FFN_CONV_WIDTH - 1, 2 * D_FF), dt_),
            *params)
        xs, s_pool, s_conv, s_h, s_ffn = hybrid_layer(
            xs, 0, PAST_LEN, state_pool[l], state_conv[l], state_ssm[l], state_ffn[l], *params)
        pp.append(p_pool); pc.append(p_conv); ph.append(p_h); pf.append(p_ffn)
        sp.append(s_pool); sc.append(s_conv); sh.append(s_h); sf.append(s_ffn)
    y_prompt = rmsnorm(xp, final_norm_w)[:, N_META:]
    y_sample = rmsnorm(xs, final_norm_w)
    return (y_prompt, y_sample,
            jnp.stack(pp), jnp.stack(pc), jnp.stack(ph), jnp.stack(pf),
            jnp.stack(sp), jnp.stack(sc), jnp.stack(sh), jnp.stack(sf))
```

```python
import functools

import jax
import jax.numpy as jnp
from jax import lax
from jax.experimental import pallas as pl
from jax.experimental.pallas import tpu as pltpu

F32 = jnp.float32
BF16 = jnp.bfloat16

D_MODEL = 1024
DEPTH = 4
N_META = 16
D_POOL = D_MODEL
POOL_WINDOWS = (2, 4, 8, 16)
POOL_GROUP_DIM = D_POOL // len(POOL_WINDOWS)
POOL_BUF = max(POOL_WINDOWS) - 1
D_INNER = 2 * D_MODEL
HEAD_DIM = 64
HEADS = D_INNER // HEAD_DIM
GROUPS = 4
GROUP_DIM = D_INNER // GROUPS
D_STATE = 128
CONV_WIDTH = 4
CONV_DIM = D_INNER + 2 * GROUPS * D_STATE
D_FF = 2816
FFN_CONV_WIDTH = 3
EPS = 1e-6
PAST_LEN = 16384

LANES = 128
SUBLANES = 8
MXU_DIM = 256
VMEM_LIMIT_BYTES = 56 * 1024 * 1024

COL_XBC = 0
COL_U = COL_XBC + CONV_DIM
COL_Z = COL_U + D_POOL
COL_GA = COL_Z + D_INNER
COL_GB = COL_GA + D_MODEL
COL_DT = COL_GB + D_MODEL
D_PROJ = COL_DT + LANES

SSD_CHUNK = 64
QUAD = MXU_DIM // SSD_CHUNK
QUAD_DIM = QUAD * HEAD_DIM
NEG_BIG = -1e30


def _params(*sem):
    return pltpu.CompilerParams(dimension_semantics=sem, vmem_limit_bytes=VMEM_LIMIT_BYTES)


def _rmsnorm(x, w):
    ms = jnp.mean(x * x, axis=-1, keepdims=True)
    return x * lax.rsqrt(ms + EPS) * w


def _silu(x):
    return x * jax.nn.sigmoid(x)


def _dot(a, b):
    return jnp.dot(a, b, preferred_element_type=F32)


def _dot_nt(a, b):
    return lax.dot_general(a, b, (((1,), (1,)), ((), ())), preferred_element_type=F32)


def _dot_tn(a, b):
    return lax.dot_general(a, b, (((0,), (0,)), ((), ())), preferred_element_type=F32)


def _split3(v):
    hi = v.astype(BF16)
    r1 = v - hi.astype(F32)
    mid = r1.astype(BF16)
    lo = (r1 - mid.astype(F32)).astype(BF16)
    return hi, mid, lo


def _cumsum_rows(mask3, v):
    parts = list(_split3(v))
    pad = mask3.shape[1] - 3 * v.shape[0]
    if pad:
        parts.append(jnp.zeros((pad, v.shape[1]), BF16))
    return _dot(mask3, jnp.concatenate(parts, axis=0))


def _expand_heads(v, expand3):
    hi, mid, lo = _split3(v)
    packed = (hi.astype(F32) + pltpu.roll(mid.astype(F32), HEADS, 1)
              + pltpu.roll(lo.astype(F32), 2 * HEADS, 1))
    return _dot(packed.astype(BF16), expand3)


def _inproj_kernel(x_ref, nw_ref, w_ref, o_ref, hn_ref):
    @pl.when(pl.program_id(1) == 0)
    def _():
        hn_ref[...] = _rmsnorm(x_ref[...], nw_ref[...]).astype(BF16)

    o_ref[...] = _dot(hn_ref[...], w_ref[...])


def _inproj(x2d, norm_w, w):
    rows = x2d.shape[0]
    tm = min(rows, 1024)
    tn = D_PROJ // 5
    return pl.pallas_call(
        _inproj_kernel,
        grid=(rows // tm, D_PROJ // tn),
        in_specs=[pl.BlockSpec((tm, D_MODEL), lambda i, j: (i, 0)),
                  pl.BlockSpec((1, D_MODEL), lambda i, j: (0, 0)),
                  pl.BlockSpec((D_MODEL, tn), lambda i, j: (0, j))],
        out_specs=pl.BlockSpec((tm, tn), lambda i, j: (i, j)),
        out_shape=jax.ShapeDtypeStruct((rows, D_PROJ), F32),
        scratch_shapes=[pltpu.VMEM((tm, D_MODEL), BF16)],
        compiler_params=_params("parallel", "arbitrary"),
        name="inproj",
    )(x2d, norm_w, w)


POOL_PAD = 2 * SUBLANES + SUBLANES


def _pool_kernel(u_ref, ga_ref, pre_ref, pw_ref, ps_ref, wo_ref, a_ref, new_ref,
                 ext_ref, lva_ref, lvb_ref, *, start_pos):
    bb, lt, _ = u_ref.shape
    top = POOL_PAD + lt
    t = pl.program_id(1)

    @pl.when(t == 0)
    def _():
        ext_ref[:, 0:POOL_PAD - POOL_BUF, :] = jnp.zeros((bb, POOL_PAD - POOL_BUF, D_POOL), F32)
        ext_ref[:, POOL_PAD - POOL_BUF:POOL_PAD, :] = pre_ref[...]
        lva_ref[:, 0:SUBLANES, :] = jnp.zeros((bb, SUBLANES, POOL_GROUP_DIM), F32)
        lvb_ref[:, 0:SUBLANES, :] = jnp.zeros((bb, SUBLANES, POOL_GROUP_DIM), F32)

    u = u_ref[...]
    ext_ref[:, POOL_PAD:top, :] = u

    if start_pos + 1 < max(POOL_WINDOWS):
        pos = start_pos + t * lt + lax.broadcasted_iota(jnp.int32, (bb, lt, POOL_GROUP_DIM), 1)

    mixed = []
    for g, win in enumerate(POOL_WINDOWS):
        c0, c1 = g * POOL_GROUP_DIM, (g + 1) * POOL_GROUP_DIM
        src, shift = ext_ref, 1
        bufs = [lva_ref, lvb_ref]
        cols = slice(c0, c1)
        while 2 * shift < win:
            dst = bufs[0]
            dst[:, SUBLANES:top, :] = (src[:, SUBLANES:top, cols]
                                       + src[:, SUBLANES - shift:top - shift, cols])
            src, cols, shift = dst, slice(0, POOL_GROUP_DIM), 2 * shift
            bufs = bufs[::-1]
        wsum = src[:, POOL_PAD:top, cols] + src[:, POOL_PAD - shift:top - shift, cols]
        ug = u[:, :, c0:c1]
        if start_pos + 1 >= max(POOL_WINDOWS):
            d = wsum * (1.0 / win) - ug
        else:
            d = wsum / jnp.minimum(pos + 1, win).astype(F32) - ug
        d = d.reshape(bb * lt, POOL_GROUP_DIM).astype(BF16)
        mixed.append((_dot(d, pw_ref[g]) * ps_ref[:, c0:c1]).astype(BF16))
    a_out = _dot(jnp.concatenate(mixed, axis=1), wo_ref[...])
    gate = jax.nn.sigmoid(ga_ref[...].reshape(bb * lt, D_MODEL))
    a_ref[...] = (gate * a_out).reshape(bb, lt, D_MODEL)

    tail = ext_ref[:, top - POOL_BUF:top, :]
    ext_ref[:, POOL_PAD - POOL_BUF:POOL_PAD, :] = tail

    @pl.when(t == pl.num_programs(1) - 1)
    def _():
        new_ref[...] = tail


def _pool_branch(proj, prefix, pool_w, pool_scale, w_pool_out, *, bb, lt, start_pos):
    b, l, _ = proj.shape
    shared = prefix.shape[0] == 1 and b > 1
    pre_map = (lambda i, t: (0, 0, 0)) if shared else (lambda i, t: (i, 0, 0))
    return pl.pallas_call(
        functools.partial(_pool_kernel, start_pos=start_pos),
        grid=(b // bb, l // lt),
        in_specs=[pl.BlockSpec((bb, lt, D_POOL), lambda i, t: (i, t, COL_U // D_POOL)),
                  pl.BlockSpec((bb, lt, D_MODEL), lambda i, t: (i, t, COL_GA // D_MODEL)),
                  pl.BlockSpec((1 if shared else bb, POOL_BUF, D_POOL), pre_map),
                  pl.BlockSpec((len(POOL_WINDOWS), POOL_GROUP_DIM, POOL_GROUP_DIM), lambda i, t: (0, 0, 0)),
                  pl.BlockSpec((1, D_POOL), lambda i, t: (0, 0)),
                  pl.BlockSpec((D_POOL, D_MODEL), lambda i, t: (0, 0))],
        out_specs=[pl.BlockSpec((bb, lt, D_MODEL), lambda i, t: (i, t, 0)),
                   pl.BlockSpec((bb, POOL_BUF, D_POOL), lambda i, t: (i, 0, 0))],
        out_shape=[jax.ShapeDtypeStruct((b, l, D_MODEL), F32),
                   jax.ShapeDtypeStruct((b, POOL_BUF, D_POOL), F32)],
        scratch_shapes=[pltpu.VMEM((bb, POOL_PAD + lt, D_POOL), F32),
                        pltpu.VMEM((bb, POOL_PAD + lt, POOL_GROUP_DIM), F32),
                        pltpu.VMEM((bb, POOL_PAD + lt, POOL_GROUP_DIM), F32)],
        compiler_params=_params("parallel", "arbitrary"),
        name="pool_branch",
    )(proj, proj, prefix, pool_w, pool_scale, w_pool_out)


CONV_PAD = SUBLANES


def _conv_silu(ext_ref, row0, n, cw_ref, cb_ref):
    acc = cb_ref[...]
    for k in range(CONV_WIDTH):
        r = row0 - (CONV_WIDTH - 1) + k
        acc = acc + ext_ref[r:r + n, :] * cw_ref[k:k + 1, :]
    return _silu(acc)


def _dt_and_decay(dt_raw, dtb_ref, alog_ref):
    lane = lax.broadcasted_iota(jnp.int32, dt_raw.shape, 1)
    v = dt_raw + dtb_ref[...]
    dt = jnp.maximum(v, 0.0) + jnp.log1p(jnp.exp(-jnp.abs(v)))
    dt = jnp.where(lane < HEADS, dt, 0.0)
    return dt, dt * (-jnp.exp(alog_ref[...]))


def _gated_norm(y, z, nw_ref):
    yf = y * _silu(z)
    outs = []
    for g in range(GROUPS):
        yg = yf[:, g * GROUP_DIM:(g + 1) * GROUP_DIM]
        outs.append(yg * lax.rsqrt(jnp.mean(yg * yg, axis=-1, keepdims=True) + EPS))
    return jnp.concatenate(outs, axis=1) * nw_ref[...]


def _ssd_prompt_kernel(xbc_ref, z_ref, dt_ref, pre_ref, h0_ref, cw_ref, cb_ref, dtb_ref, alog_ref,
                       dsk_ref, nw_ref, exp3_ref, tri3_ref, y_ref, cnew_ref, hnew_ref,
                       ext_ref, act_ref, ht_ref):
    tq = xbc_ref.shape[1]
    q = SSD_CHUNK
    t = pl.program_id(1)

    @pl.when(t == 0)
    def _():
        ext_ref[CONV_PAD - (CONV_WIDTH - 1):CONV_PAD, :] = pre_ref[0]
        ht_ref[...] = h0_ref[0]

    ext_ref[CONV_PAD:CONV_PAD + tq, :] = xbc_ref[0]
    for c in range(tq // q):
        act_ref[c * q:(c + 1) * q, :] = _conv_silu(ext_ref, CONV_PAD + c * q, q, cw_ref, cb_ref)
    tail = ext_ref[CONV_PAD + tq - (CONV_WIDTH - 1):CONV_PAD + tq, :]
    ext_ref[CONV_PAD - (CONV_WIDTH - 1):CONV_PAD, :] = tail

    row = lax.broadcasted_iota(jnp.int32, (q, QUAD_DIM), 0)
    col = lax.broadcasted_iota(jnp.int32, (q, QUAD_DIM), 1) % q
    causal = row >= col
    diag = row == col
    brow = lax.broadcasted_iota(jnp.int32, (QUAD_DIM, QUAD_DIM), 0) // q
    bcol = lax.broadcasted_iota(jnp.int32, (QUAD_DIM, QUAD_DIM), 1) // HEAD_DIM
    blockdiag = brow == bcol

    def chunk(c, carry):
        r0 = pl.multiple_of(c * q, q)
        act = act_ref[pl.ds(r0, q), :]
        xs = act[:, :D_INNER]
        bm = act[:, D_INNER:D_INNER + GROUPS * D_STATE].astype(BF16)
        cm = act[:, D_INNER + GROUPS * D_STATE:].astype(BF16)
        dt, da = _dt_and_decay(dt_ref[0, pl.ds(r0, q), :], dtb_ref, alog_ref)
        cum = _cumsum_rows(tri3_ref[...], da)
        cum_e = _expand_heads(cum, exp3_ref[...])
        dt_e = _expand_heads(dt, exp3_ref[...])
        last_e = cum_e[q - 1:q, :]
        xdt = xs * dt_e
        xdt_b = xdt.astype(BF16)
        xw_b = (xdt * jnp.exp(last_e - cum_e)).astype(BF16)
        ht = ht_ref[...]
        ht_b = ht.astype(BF16)

        y_parts, h_parts = [], []
        for g in range(GROUPS):
            bm_g = bm[:, g * D_STATE:(g + 1) * D_STATE]
            cm_g = cm[:, g * D_STATE:(g + 1) * D_STATE]
            scores = _dot_nt(cm_g, jnp.concatenate([bm_g] * QUAD, axis=0))
            for k in range(GROUP_DIM // QUAD_DIM):
                c0 = g * GROUP_DIM + k * QUAD_DIM
                cum_k = cum_e[:, c0:c0 + QUAD_DIM]
                cum_row = jnp.sum(jnp.where(diag, cum_k, 0.0), axis=0, keepdims=True)
                decay = jnp.exp(jnp.where(causal, cum_k - cum_row, NEG_BIG))
                m = (scores * decay).astype(BF16)
                xk = jnp.concatenate([xdt_b[:, c0:c0 + QUAD_DIM]] * QUAD, axis=0)
                xk = jnp.where(blockdiag, xk, jnp.zeros_like(xk))
                y_parts.append(_dot(m, xk))
            g0 = g * GROUP_DIM
            h_parts.append((_dot(cm_g, ht_b[:, g0:g0 + GROUP_DIM]),
                            _dot_tn(bm_g, xw_b[:, g0:g0 + GROUP_DIM])))
        y_intra = jnp.concatenate(y_parts, axis=1)
        y_inter = jnp.concatenate([p[0] for p in h_parts], axis=1)
        h_add = jnp.concatenate([p[1] for p in h_parts], axis=1)
        ht_ref[...] = ht * jnp.exp(last_e) + h_add
        y = y_intra + y_inter * jnp.exp(cum_e) + dsk_ref[...] * xs
        y_ref[0, pl.ds(r0, q), :] = _gated_norm(y, z_ref[0, pl.ds(r0, q), :], nw_ref).astype(BF16)
        return carry

    lax.fori_loop(0, tq // q, chunk, 0)

    @pl.when(t == pl.num_programs(1) - 1)
    def _():
        cnew_ref[0] = tail
        hnew_ref[0] = ht_ref[...].T


def _ssd_prompt(proj, prefix, h0t, lw, consts, *, tq):
    b, l, _ = proj.shape
    q = SSD_CHUNK
    full = lambda *shape: pl.BlockSpec(shape, lambda i, t: (0,) * len(shape))
    return pl.pallas_call(
        _ssd_prompt_kernel,
        grid=(b, l // tq),
        in_specs=[pl.BlockSpec((1, tq, CONV_DIM), lambda i, t: (i, t, COL_XBC // CONV_DIM)),
                  pl.BlockSpec((1, tq, D_INNER), lambda i, t: (i, t, COL_Z // D_INNER)),
                  pl.BlockSpec((1, tq, LANES), lambda i, t: (i, t, COL_DT // LANES)),
                  full(1, CONV_WIDTH - 1, CONV_DIM),
                  full(1, D_STATE, D_INNER),
                  full(CONV_WIDTH, CONV_DIM), full(1, CONV_DIM), full(1, LANES), full(1, LANES),
                  full(1, D_INNER), full(1, D_INNER), full(LANES, D_INNER), full(*consts["tri3_prompt"].shape)],
        out_specs=[pl.BlockSpec((1, tq, D_INNER), lambda i, t: (i, t, 0)),
                   pl.BlockSpec((1, CONV_WIDTH - 1, CONV_DIM), lambda i, t: (i, 0, 0)),
                   pl.BlockSpec((1, D_INNER, D_STATE), lambda i, t: (i, 0, 0))],
        out_shape=[jax.ShapeDtypeStruct((b, l, D_INNER), BF16),
                   jax.ShapeDtypeStruct((b, CONV_WIDTH - 1, CONV_DIM), F32),
                   jax.ShapeDtypeStruct((b, D_INNER, D_STATE), F32)],
        scratch_shapes=[pltpu.VMEM((CONV_PAD + tq, CONV_DIM), F32),
                        pltpu.VMEM((tq, CONV_DIM), F32),
                        pltpu.VMEM((D_STATE, D_INNER), F32)],
        compiler_params=_params("parallel", "arbitrary"),
        name="ssd_prompt",
    )(proj, proj, proj, prefix, h0t, lw["conv_w"], lw["conv_b"], lw["dt_bias"], lw["a_log"],
      lw["d_skip"], lw["ssd_norm_w"], consts["expand3"], consts["tri3_prompt"])


def _ssd_short_kernel(xbc_ref, z_ref, dt_ref, pre_ref, h0_ref, cw_ref, cb_ref, dtb_ref, alog_ref,
                      dsk_ref, nw_ref, exp3_ref, tri3_ref, y_ref, cnew_ref, hnew_ref, *rest,
                      emit_transposed):
    if emit_transposed:
        hnewt_ref, ext_ref, cume_ref, dte_ref = rest
    else:
        ext_ref, cume_ref, dte_ref = rest
    bb, q, _ = xbc_ref.shape

    dt, da = _dt_and_decay(dt_ref[...].reshape(bb * q, LANES), dtb_ref, alog_ref)
    cume_ref[...] = _expand_heads(_cumsum_rows(tri3_ref[...], da), exp3_ref[...])
    dte_ref[...] = _expand_heads(dt, exp3_ref[...])
    trow = lax.broadcasted_iota(jnp.int32, (q, D_INNER), 0)

    def one(i, carry):
        ext_ref[CONV_PAD - (CONV_WIDTH - 1):CONV_PAD, :] = pre_ref[i]
        ext_ref[CONV_PAD:CONV_PAD + q, :] = xbc_ref[i]
        cnew_ref[i] = ext_ref[CONV_PAD + q - (CONV_WIDTH - 1):CONV_PAD + q, :]
        act = _conv_silu(ext_ref, CONV_PAD, q, cw_ref, cb_ref)
        xs = act[:, :D_INNER]
        bm = act[:, D_INNER:D_INNER + GROUPS * D_STATE]
        cm = act[:, D_INNER + GROUPS * D_STATE:]
        r0 = pl.multiple_of(i * q, q)
        cum_e = cume_ref[pl.ds(r0, q), :]
        dt_e = dte_ref[pl.ds(r0, q), :]
        last_e = cum_e[q - 1:q, :]
        xdt = xs * dt_e
        xw_b = (xdt * jnp.exp(last_e - cum_e)).astype(BF16)
        ht = h0_ref[i].T
        ht_b = ht.astype(BF16)

        y = dsk_ref[...] * xs
        for s in range(q):
            decay = jnp.exp(jnp.where(trow >= s, cum_e - cum_e[s:s + 1, :], NEG_BIG))
            coef = []
            for g in range(GROUPS):
                sl = slice(g * D_STATE, (g + 1) * D_STATE)
                sc = jnp.sum(cm[:, sl] * bm[s:s + 1, sl], axis=-1, keepdims=True)
                coef.append(jnp.broadcast_to(sc, (q, GROUP_DIM)))
            y = y + jnp.concatenate(coef, axis=1) * decay * xdt[s:s + 1, :]

        bm_b, cm_b = bm.astype(BF16), cm.astype(BF16)
        y_inter, h_add = [], []
        for g in range(GROUPS):
            sl = slice(g * D_STATE, (g + 1) * D_STATE)
            gl = slice(g * GROUP_DIM, (g + 1) * GROUP_DIM)
            y_inter.append(_dot(cm_b[:, sl], ht_b[:, gl]))
            h_add.append(_dot_tn(bm_b[:, sl], xw_b[:, gl]))
        y = y + jnp.concatenate(y_inter, axis=1) * jnp.exp(cum_e)
        ht_new = ht * jnp.exp(last_e) + jnp.concatenate(h_add, axis=1)
        hnew_ref[i] = ht_new.T
        if emit_transposed:
            hnewt_ref[i] = ht_new
        y_ref[i] = _gated_norm(y, z_ref[i], nw_ref).astype(BF16)
        return carry

    lax.fori_loop(0, bb, one, 0)


def _ssd_short(proj, prefix, h0, lw, consts, tri3, *, bb, emit_transposed):
    b, q, _ = proj.shape
    full = lambda *shape: pl.BlockSpec(shape, lambda i: (0,) * len(shape))
    out_specs = [pl.BlockSpec((bb, q, D_INNER), lambda i: (i, 0, 0)),
                 pl.BlockSpec((bb, CONV_WIDTH - 1, CONV_DIM), lambda i: (i, 0, 0)),
                 pl.BlockSpec((bb, D_INNER, D_STATE), lambda i: (i, 0, 0))]
    out_shape = [jax.ShapeDtypeStruct((b, q, D_INNER), BF16),
                 jax.ShapeDtypeStruct((b, CONV_WIDTH - 1, CONV_DIM), F32),
                 jax.ShapeDtypeStruct((b, D_INNER, D_STATE), F32)]
    if emit_transposed:
        out_specs.append(pl.BlockSpec((bb, D_STATE, D_INNER), lambda i: (i, 0, 0)))
        out_shape.append(jax.ShapeDtypeStruct((b, D_STATE, D_INNER), F32))
    return pl.pallas_call(
        functools.partial(_ssd_short_kernel, emit_transposed=emit_transposed),
        grid=(b // bb,),
        in_specs=[pl.BlockSpec((bb, q, CONV_DIM), lambda i: (i, 0, COL_XBC // CONV_DIM)),
                  pl.BlockSpec((bb, q, D_INNER), lambda i: (i, 0, COL_Z // D_INNER)),
                  pl.BlockSpec((bb, q, LANES), lambda i: (i, 0, COL_DT // LANES)),
                  pl.BlockSpec((bb, CONV_WIDTH - 1, CONV_DIM), lambda i: (i, 0, 0)),
                  pl.BlockSpec((bb, D_INNER, D_STATE), lambda i: (i, 0, 0)),
                  full(CONV_WIDTH, CONV_DIM), full(1, CONV_DIM), full(1, LANES), full(1, LANES),
                  full(1, D_INNER), full(1, D_INNER), full(LANES, D_INNER), full(*tri3.shape)],
        out_specs=out_specs,
        out_shape=out_shape,
        scratch_shapes=[pltpu.VMEM((CONV_PAD + q, CONV_DIM), F32),
                        pltpu.VMEM((bb * q, D_INNER), F32),
                        pltpu.VMEM((bb * q, D_INNER), F32)],
        compiler_params=_params("parallel"),
        name="ssd_short",
    )(proj, proj, proj, prefix, h0, lw["conv_w"], lw["conv_b"], lw["dt_bias"], lw["a_log"],
      lw["d_skip"], lw["ssd_norm_w"], consts["expand3"], tri3)


def _outproj_kernel(y_ref, a_ref, gb_ref, x_ref, ws_ref, wo_ref, o_ref):
    b_out = _dot(y_ref[...], ws_ref[...])
    merged = a_ref[...] + jax.nn.sigmoid(gb_ref[...]) * b_out
    o_ref[...] = x_ref[...] + _dot(merged.astype(BF16), wo_ref[...])


def _outproj(y2d, a2d, proj2d, x2d, w_ssd_out, w_o):
    rows = x2d.shape[0]
    tm = min(rows, 1024)
    return pl.pallas_call(
        _outproj_kernel,
        grid=(rows // tm,),
        in_specs=[pl.BlockSpec((tm, D_INNER), lambda i: (i, 0)),
                  pl.BlockSpec((tm, D_MODEL), lambda i: (i, 0)),
                  pl.BlockSpec((tm, D_MODEL), lambda i: (i, COL_GB // D_MODEL)),
                  pl.BlockSpec((tm, D_MODEL), lambda i: (i, 0)),
                  pl.BlockSpec((D_INNER, D_MODEL), lambda i: (0, 0)),
                  pl.BlockSpec((D_MODEL, D_MODEL), lambda i: (0, 0))],
        out_specs=pl.BlockSpec((tm, D_MODEL), lambda i: (i, 0)),
        out_shape=jax.ShapeDtypeStruct((rows, D_MODEL), F32),
        compiler_params=_params("parallel"),
        name="outproj",
    )(y2d, a2d, proj2d, x2d, w_ssd_out, w_o)


FFN_PAD = SUBLANES


def _ffn_kernel(x_ref, pre_ref, nw_ref, wu_ref, cw_ref, cb_ref, wd_ref, o_ref, new_ref, ext_ref):
    bb, lt, _ = x_ref.shape
    top = FFN_PAD + lt
    keep = FFN_CONV_WIDTH - 1
    t = pl.program_id(1)

    @pl.when(t == 0)
    def _():
        ext_ref[:, FFN_PAD - keep:FFN_PAD, :] = pre_ref[...]

    x = x_ref[...].reshape(bb * lt, D_MODEL)
    hn = _rmsnorm(x, nw_ref[...]).astype(BF16)
    ext_ref[:, FFN_PAD:top, :] = _dot(hn, wu_ref[...]).reshape(bb, lt, 2 * D_FF)
    tail = ext_ref[:, top - keep:top, :]

    acc_g = cb_ref[:, :D_FF]
    acc_v = cb_ref[:, D_FF:]
    for k in range(FFN_CONV_WIDTH):
        r = FFN_PAD - keep + k
        acc_g = acc_g + ext_ref[:, r:r + lt, :D_FF] * cw_ref[k:k + 1, :D_FF]
        acc_v = acc_v + ext_ref[:, r:r + lt, D_FF:] * cw_ref[k:k + 1, D_FF:]
    act = (_silu(acc_g) * acc_v).reshape(bb * lt, D_FF).astype(BF16)
    o_ref[...] = (x + _dot(act, wd_ref[...])).reshape(bb, lt, D_MODEL)

    ext_ref[:, FFN_PAD - keep:FFN_PAD, :] = tail

    @pl.when(t == pl.num_programs(1) - 1)
    def _():
        new_ref[...] = tail


def _ffn(x, prefix, norm_w, w_up, conv_w, conv_b, w_down, *, bb, lt):
    b, l, _ = x.shape
    keep = FFN_CONV_WIDTH - 1
    shared = prefix.shape[0] == 1 and b > 1
    pre_map = (lambda i, t: (0, 0, 0)) if shared else (lambda i, t: (i, 0, 0))
    full = lambda *shape: pl.BlockSpec(shape, lambda i, t: (0,) * len(shape))
    return pl.pallas_call(
        _ffn_kernel,
        grid=(b // bb, l // lt),
        in_specs=[pl.BlockSpec((bb, lt, D_MODEL), lambda i, t: (i, t, 0)),
                  pl.BlockSpec((1 if shared else bb, keep, 2 * D_FF), pre_map),
                  full(1, D_MODEL), full(D_MODEL, 2 * D_FF), full(FFN_CONV_WIDTH, 2 * D_FF),
                  full(1, 2 * D_FF), full(D_FF, D_MODEL)],
        out_specs=[pl.BlockSpec((bb, lt, D_MODEL), lambda i, t: (i, t, 0)),
                   pl.BlockSpec((bb, keep, 2 * D_FF), lambda i, t: (i, 0, 0))],
        out_shape=[jax.ShapeDtypeStruct((b, l, D_MODEL), F32),
                   jax.ShapeDtypeStruct((b, keep, 2 * D_FF), F32)],
        scratch_shapes=[pltpu.VMEM((bb, FFN_PAD + lt, 2 * D_FF), F32)],
        compiler_params=_params("parallel", "arbitrary"),
        name="conv_ffn",
    )(x, prefix, norm_w, w_up, conv_w, conv_b, w_down)


def _final_norm_kernel(x_ref, w_ref, o_ref):
    o_ref[...] = _rmsnorm(x_ref[...], w_ref[...])


def _final_norm(x2d, w):
    rows = x2d.shape[0]
    tm = min(rows, 1024)
    return pl.pallas_call(
        _final_norm_kernel,
        grid=(rows // tm,),
        in_specs=[pl.BlockSpec((tm, D_MODEL), lambda i: (i, 0)),
                  pl.BlockSpec((1, D_MODEL), lambda i: (0, 0))],
        out_specs=pl.BlockSpec((tm, D_MODEL), lambda i: (i, 0)),
        out_shape=jax.ShapeDtypeStruct((rows, D_MODEL), F32),
        compiler_params=_params("parallel"),
        name="final_norm",
    )(x2d, w)


STREAMS = {
    "meta": dict(pool=(1, N_META), ffn=(1, N_META), ssd_bb=1, start_pos=0),
    "prompt": dict(pool=(1, 512), ffn=(1, 256), ssd_tq=256, start_pos=N_META),
    "sample": dict(pool=(32, 8), ffn=(16, 8), ssd_bb=8, start_pos=PAST_LEN),
}


def _causal_blocks3(rows, q):
    i = jnp.arange(rows)[:, None]
    j = jnp.arange(rows)[None, :]
    m = ((i // q == j // q) & (j <= i)).astype(BF16)
    m3 = jnp.concatenate([m, m, m], axis=1)
    return jnp.pad(m3, ((0, 0), (0, -m3.shape[1] % LANES)))


def _constants():
    r = jnp.arange(LANES)[:, None]
    c = jnp.arange(D_INNER)[None, :]
    expand3 = ((r < 3 * HEADS) & (r % HEADS == c // HEAD_DIM)).astype(BF16)
    return dict(
        expand3=expand3,
        tri3_prompt=_causal_blocks3(SSD_CHUNK, SSD_CHUNK),
        tri3_meta=_causal_blocks3(N_META, N_META),
        tri3_sample=_causal_blocks3(STREAMS["sample"]["ssd_bb"] * 8, 8),
    )


def _layer(name, x, state, lw, consts):
    cfg = STREAMS[name]
    b, l, _ = x.shape
    pool_pre, conv_pre, ssm0, ffn_pre = state
    proj = _inproj(x.reshape(b * l, D_MODEL), lw["norm1_w"], lw["w_in"]).reshape(b, l, D_PROJ)
    a_gated, pool_new = _pool_branch(proj, pool_pre, lw["pool_w"], lw["pool_scale"], lw["w_pool_out"],
                                     bb=cfg["pool"][0], lt=cfg["pool"][1], start_pos=cfg["start_pos"])
    if name == "prompt":
        y, conv_new, ssm_new = _ssd_prompt(proj, conv_pre, ssm0, lw, consts, tq=cfg["ssd_tq"])
    else:
        outs = _ssd_short(proj, conv_pre, ssm0, lw, consts, consts["tri3_" + name],
                          bb=cfg["ssd_bb"], emit_transposed=(name == "meta"))
        y, conv_new, ssm_new = outs[0], outs[1], outs[2:]
        ssm_new = ssm_new[-1]
    x1 = _outproj(y.reshape(b * l, D_INNER), a_gated.reshape(b * l, D_MODEL),
                  proj.reshape(b * l, D_PROJ), x.reshape(b * l, D_MODEL),
                  lw["w_ssd_out"], lw["w_o"]).reshape(b, l, D_MODEL)
    x2, ffn_new = _ffn(x1, ffn_pre, lw["norm2_w"], lw["w_up"], lw["ffn_conv_w"], lw["ffn_conv_b"],
                       lw["w_down"], bb=cfg["ffn"][0], lt=cfg["ffn"][1])
    return x2, (pool_new, conv_new, ssm_new, ffn_new)


def kernel(x_prompt, x_sample, state_pool, state_conv, state_ssm, state_ffn, meta_tokens, norm1_w, w_in, pool_w, pool_scale, w_pool_out, conv_w, conv_b, dt_bias, a_log, d_skip, ssd_norm_w, w_ssd_out, w_o, norm2_w, w_up, ffn_conv_w, ffn_conv_b, w_down, final_norm_w):
    b_p = x_prompt.shape[0]
    b_s = x_sample.shape[0]

    w_in_r = jnp.concatenate([
        w_in[..., 3072:6144], w_in[..., 0:1024], w_in[..., 1024:3072],
        w_in[..., 6176:7200], w_in[..., 7200:8224], w_in[..., 6144:6176],
        jnp.zeros((DEPTH, D_MODEL, LANES - HEADS), F32)], axis=-1).astype(BF16)
    pad_heads = lambda v: jnp.pad(v, ((0, 0), (0, LANES - HEADS)))[:, None, :]
    weights = dict(
        norm1_w=norm1_w[:, None, :], w_in=w_in_r, pool_w=pool_w.astype(BF16),
        pool_scale=pool_scale[:, None, :], w_pool_out=w_pool_out.astype(BF16),
        conv_w=conv_w, conv_b=conv_b[:, None, :], dt_bias=pad_heads(dt_bias), a_log=pad_heads(a_log),
        d_skip=jnp.repeat(d_skip, HEAD_DIM, axis=-1)[:, None, :], ssd_norm_w=ssd_norm_w[:, None, :],
        w_ssd_out=w_ssd_out.astype(BF16), w_o=w_o.astype(BF16), norm2_w=norm2_w[:, None, :],
        w_up=w_up.astype(BF16), ffn_conv_w=ffn_conv_w, ffn_conv_b=ffn_conv_b[:, None, :],
        w_down=w_down.astype(BF16))
    consts = _constants()

    xm = meta_tokens.astype(x_prompt.dtype)[None]
    xp = x_prompt
    xs = x_sample
    meta_state0 = (jnp.zeros((1, POOL_BUF, D_POOL), F32), jnp.zeros((1, CONV_WIDTH - 1, CONV_DIM), F32),
                   jnp.zeros((1, D_INNER, D_STATE), F32), jnp.zeros((1, FFN_CONV_WIDTH - 1, 2 * D_FF), F32))
    p_states, s_states = [], []
    for l in range(DEPTH):
        lw = {k: v[l] for k, v in weights.items()}
        xm, m_state = _layer("meta", xm, meta_state0, lw, consts)
        xp, p_state = _layer("prompt", xp, m_state, lw, consts)
        ssm_in = state_ssm[l].reshape(b_s, D_INNER, D_STATE)
        xs, s_state = _layer("sample", xs, (state_pool[l], state_conv[l], ssm_in, state_ffn[l]), lw, consts)
        p_states.append(p_state)
        s_states.append(s_state)

    fw = final_norm_w[None, :]
    y_prompt = _final_norm(xp.reshape(-1, D_MODEL), fw).reshape(xp.shape)
    y_sample = _final_norm(xs.reshape(-1, D_MODEL), fw).reshape(xs.shape)

    def stack(states, idx, shape):
        return jnp.stack([s[idx] for s in states]).reshape(shape)

    ssm_shape = lambda b: (DEPTH, b, HEADS, HEAD_DIM, D_STATE)
    return (y_prompt, y_sample,
            stack(p_states, 0, (DEPTH, b_p, POOL_BUF, D_POOL)),
            stack(p_states, 1, (DEPTH, b_p, CONV_WIDTH - 1, CONV_DIM)),
            stack(p_states, 2, ssm_shape(b_p)),
            stack(p_states, 3, (DEPTH, b_p, FFN_CONV_WIDTH - 1, 2 * D_FF)),
            stack(s_states, 0, (DEPTH, b_s, POOL_BUF, D_POOL)),
            stack(s_states, 1, (DEPTH, b_s, CONV_WIDTH - 1, CONV_DIM)),
            stack(s_states, 2, ssm_shape(b_s)),
            stack(s_states, 3, (DEPTH, b_s, FFN_CONV_WIDTH - 1, 2 * D_FF)))
```

```python
import functools

import jax
import jax.numpy as jnp
from jax import lax
from jax.experimental import pallas as pl
from jax.experimental.pallas import tpu as pltpu

F32 = jnp.float32
BF16 = jnp.bfloat16

D_MODEL = 1024
DEPTH = 4
N_META = 16
D_POOL = D_MODEL
POOL_WINDOWS = (2, 4, 8, 16)
POOL_GROUP_DIM = D_POOL // len(POOL_WINDOWS)
POOL_BUF = max(POOL_WINDOWS) - 1
D_INNER = 2 * D_MODEL
HEAD_DIM = 64
HEADS = D_INNER // HEAD_DIM
GROUPS = 4
GROUP_DIM = D_INNER // GROUPS
D_STATE = 128
CONV_WIDTH = 4
CONV_KEEP = CONV_WIDTH - 1
CONV_DIM = D_INNER + 2 * GROUPS * D_STATE
D_FF = 2816
FFN_CONV_WIDTH = 3
FFN_KEEP = FFN_CONV_WIDTH - 1
EPS = 1e-6
PAST_LEN = 16384

LANES = 128
SUBLANES = 8
MXU_DIM = 256
VMEM_LIMIT_BYTES = 56 * 1024 * 1024

COL_XBC = 0
COL_U = COL_XBC + CONV_DIM
COL_Z = COL_U + D_POOL
COL_GA = COL_Z + D_INNER
COL_GB = COL_GA + D_MODEL
COL_DT = COL_GB + D_MODEL
D_PROJ = COL_DT + LANES

SSD_CHUNK = 64
QUAD = MXU_DIM // SSD_CHUNK
QUAD_DIM = QUAD * HEAD_DIM
CONV_SLABS = CONV_DIM // LANES
XS_SLABS = D_INNER // LANES
GROUP_SLABS = GROUP_DIM // LANES
CONV_STRIDE = 4
NEG_BIG = -1e30


def _params(*sem):
    return pltpu.CompilerParams(dimension_semantics=sem, vmem_limit_bytes=VMEM_LIMIT_BYTES)


def _layer_spec(layer, shape, index_map):
    return pl.BlockSpec((None,) + tuple(shape), lambda *g: (layer,) + tuple(index_map(*g)))


def _whole(layer, arr):
    shape = arr.shape[1:]
    return _layer_spec(layer, shape, lambda *g: (0,) * len(shape))


def _rmsnorm(x, w):
    ms = jnp.mean(x * x, axis=-1, keepdims=True)
    return x * lax.rsqrt(ms + EPS) * w


def _silu(x):
    return x * jax.nn.sigmoid(x)


def _dot(a, b):
    return jnp.dot(a, b, preferred_element_type=F32)


def _dot_nt(a, b):
    return lax.dot_general(a, b, (((1,), (1,)), ((), ())), preferred_element_type=F32)


def _dot_tn(a, b):
    return lax.dot_general(a, b, (((0,), (0,)), ((), ())), preferred_element_type=F32)


def _split3(v):
    hi = v.astype(BF16)
    r1 = v - hi.astype(F32)
    mid = r1.astype(BF16)
    lo = (r1 - mid.astype(F32)).astype(BF16)
    return hi, mid, lo


def _cumsum_rows(mask3, v):
    parts = list(_split3(v))
    pad = mask3.shape[1] - 3 * v.shape[0]
    if pad:
        parts.append(jnp.zeros((pad, v.shape[1]), BF16))
    return _dot(mask3, jnp.concatenate(parts, axis=0))


def _pack3(v):
    lane = lax.broadcasted_iota(jnp.int32, v.shape, 1)
    hi, mid, lo = _split3(jnp.where(lane < HEADS, v, 0.0))
    packed = (hi.astype(F32) + pltpu.roll(mid.astype(F32), HEADS, 1)
              + pltpu.roll(lo.astype(F32), 2 * HEADS, 1))
    return packed.astype(BF16)


def _expand_heads(v, expand3):
    return _dot(_pack3(v), expand3)


def _inproj_kernel(x_ref, nw_ref, w_ref, o_ref, hn_ref):
    @pl.when(pl.program_id(1) == 0)
    def _():
        hn_ref[...] = _rmsnorm(x_ref[...], nw_ref[...]).astype(BF16)

    o_ref[...] = _dot(hn_ref[...], w_ref[...])


def _inproj(x2d, lw, layer):
    rows = x2d.shape[0]
    tm = min(rows, 1024)
    tn = D_PROJ // 5
    return pl.pallas_call(
        _inproj_kernel,
        grid=(rows // tm, D_PROJ // tn),
        in_specs=[pl.BlockSpec((tm, D_MODEL), lambda i, j: (i, 0)),
                  _whole(layer, lw["norm1_w"]),
                  _layer_spec(layer, (D_MODEL, tn), lambda i, j: (0, j))],
        out_specs=pl.BlockSpec((tm, tn), lambda i, j: (i, j)),
        out_shape=jax.ShapeDtypeStruct((rows, D_PROJ), F32),
        scratch_shapes=[pltpu.VMEM((tm, D_MODEL), BF16)],
        compiler_params=_params("parallel", "arbitrary"),
        name="inproj",
    )(x2d, lw["norm1_w"], lw["w_in"])


POOL_PAD = 3 * SUBLANES


def _pool_kernel(u_ref, ga_ref, pre_ref, pw_ref, ps_ref, wo_ref, a_ref, new_ref,
                 ext_ref, lva_ref, lvb_ref, *, start_pos):
    bb, lt, _ = u_ref.shape
    top = POOL_PAD + lt
    t = pl.program_id(1)

    @pl.when(t == 0)
    def _():
        ext_ref[:, 0:POOL_PAD - POOL_BUF, :] = jnp.zeros((bb, POOL_PAD - POOL_BUF, D_POOL), F32)
        ext_ref[:, POOL_PAD - POOL_BUF:POOL_PAD, :] = pre_ref[...]
        lva_ref[:, 0:SUBLANES, :] = jnp.zeros((bb, SUBLANES, POOL_GROUP_DIM), F32)
        lvb_ref[:, 0:SUBLANES, :] = jnp.zeros((bb, SUBLANES, POOL_GROUP_DIM), F32)

    u = u_ref[...]
    ext_ref[:, POOL_PAD:top, :] = u

    if start_pos + 1 < max(POOL_WINDOWS):
        pos = start_pos + t * lt + lax.broadcasted_iota(jnp.int32, (bb, lt, POOL_GROUP_DIM), 1)

    mixed = []
    for g, win in enumerate(POOL_WINDOWS):
        c0, c1 = g * POOL_GROUP_DIM, (g + 1) * POOL_GROUP_DIM
        src, shift = ext_ref, 1
        bufs = [lva_ref, lvb_ref]
        cols = slice(c0, c1)
        while 2 * shift < win:
            dst = bufs[0]
            dst[:, SUBLANES:top, :] = (src[:, SUBLANES:top, cols]
                                       + src[:, SUBLANES - shift:top - shift, cols])
            src, cols, shift = dst, slice(0, POOL_GROUP_DIM), 2 * shift
            bufs = bufs[::-1]
        wsum = src[:, POOL_PAD:top, cols] + src[:, POOL_PAD - shift:top - shift, cols]
        ug = u[:, :, c0:c1]
        if start_pos + 1 >= max(POOL_WINDOWS):
            d = wsum * (1.0 / win) - ug
        else:
            d = wsum / jnp.minimum(pos + 1, win).astype(F32) - ug
        d = d.reshape(bb * lt, POOL_GROUP_DIM).astype(BF16)
        mixed.append((_dot(d, pw_ref[g]) * ps_ref[:, c0:c1]).astype(BF16))
    a_out = _dot(jnp.concatenate(mixed, axis=1), wo_ref[...])
    gate = jax.nn.sigmoid(ga_ref[...].reshape(bb * lt, D_MODEL))
    a_ref[...] = (gate * a_out).reshape(bb, lt, D_MODEL)

    tail = ext_ref[:, top - POOL_BUF:top, :]
    ext_ref[:, POOL_PAD - POOL_BUF:POOL_PAD, :] = tail

    @pl.when(t == pl.num_programs(1) - 1)
    def _():
        new_ref[...] = tail


def _pool_branch(proj, prefix, pre_layer, lw, layer, *, bb, lt, start_pos):
    b, l, _ = proj.shape
    pb = prefix.shape[1]
    pre_spec = _layer_spec(pre_layer, (bb if pb == b else 1, POOL_BUF, D_POOL),
                           lambda i, t: (i if pb == b else 0, 0, 0))
    return pl.pallas_call(
        functools.partial(_pool_kernel, start_pos=start_pos),
        grid=(b // bb, l // lt),
        in_specs=[pl.BlockSpec((bb, lt, D_POOL), lambda i, t: (i, t, COL_U // D_POOL)),
                  pl.BlockSpec((bb, lt, D_MODEL), lambda i, t: (i, t, COL_GA // D_MODEL)),
                  pre_spec,
                  _whole(layer, lw["pool_w"]), _whole(layer, lw["pool_scale"]),
                  _whole(layer, lw["w_pool_out"])],
        out_specs=[pl.BlockSpec((bb, lt, D_MODEL), lambda i, t: (i, t, 0)),
                   pl.BlockSpec((bb, POOL_BUF, D_POOL), lambda i, t: (i, 0, 0))],
        out_shape=[jax.ShapeDtypeStruct((b, l, D_MODEL), F32),
                   jax.ShapeDtypeStruct((b, POOL_BUF, D_POOL), F32)],
        scratch_shapes=[pltpu.VMEM((bb, POOL_PAD + lt, D_POOL), F32),
                        pltpu.VMEM((bb, POOL_PAD + lt, POOL_GROUP_DIM), F32),
                        pltpu.VMEM((bb, POOL_PAD + lt, POOL_GROUP_DIM), F32)],
        compiler_params=_params("parallel", "arbitrary"),
        name="pool_branch",
    )(proj, proj, prefix, lw["pool_w"], lw["pool_scale"], lw["w_pool_out"])


CONV_PAD = SUBLANES


def _dt_and_decay(dt_raw, dtb_ref, alog_ref):
    lane = lax.broadcasted_iota(jnp.int32, dt_raw.shape, 1)
    v = dt_raw + dtb_ref[...]
    dt = jnp.maximum(v, 0.0) + jnp.log1p(jnp.exp(-jnp.abs(v)))
    dt = jnp.where(lane < HEADS, dt, 0.0)
    return dt, dt * (-jnp.exp(alog_ref[...]))


def _gated_norm_group(y, z, w):
    yf = y * _silu(z)
    return yf * lax.rsqrt(jnp.mean(yf * yf, axis=-1, keepdims=True) + EPS) * w


def _ssd_prompt_kernel(xbc_ref, z_ref, dt_ref, pre_ref, h0_ref, cw_ref, cb_ref, dtb_ref, alog_ref,
                       dsk_ref, nw_ref, exp3_ref, tri3_ref, y_ref, cnew_ref, hnew_ref,
                       xslab_ref, act_ref, ht_ref):
    tq = xbc_ref.shape[1]
    q = SSD_CHUNK
    t = pl.program_id(1)
    tail_rows = slice(CONV_PAD + tq - CONV_KEEP, CONV_PAD + tq)
    lanes_of = lambda c: slice(c * LANES, (c + 1) * LANES)

    @pl.when(t == 0)
    def _():
        for c in range(CONV_SLABS):
            xslab_ref[c, CONV_PAD - CONV_KEEP:CONV_PAD, :] = pre_ref[0, :, lanes_of(c)]
        ht_ref[...] = h0_ref[0]

    for c in range(CONV_SLABS):
        xslab_ref[c, CONV_PAD:CONV_PAD + tq, :] = xbc_ref[0, :, lanes_of(c)]

    n_str = tq // CONV_STRIDE

    def conv_slab(c, carry):
        w = cw_ref[c]
        taps = {off: xslab_ref[c, pl.ds(CONV_PAD + off, n_str, stride=CONV_STRIDE), :]
                for off in range(-CONV_KEEP, CONV_STRIDE)}
        for j in range(CONV_STRIDE):
            acc = cb_ref[c]
            for k in range(CONV_WIDTH):
                acc = acc + taps[j - CONV_KEEP + k] * w[k:k + 1, :]
            act_ref[c, pl.ds(j, n_str, stride=CONV_STRIDE), :] = _silu(acc)
        return carry

    lax.fori_loop(0, CONV_SLABS, conv_slab, 0)

    @pl.when(t == pl.num_programs(1) - 1)
    def _():
        for c in range(CONV_SLABS):
            cnew_ref[0, :, lanes_of(c)] = xslab_ref[c, tail_rows, :]

    xslab_ref[:, CONV_PAD - CONV_KEEP:CONV_PAD, :] = xslab_ref[:, tail_rows, :]

    row = lax.broadcasted_iota(jnp.int32, (q, QUAD_DIM), 0)
    col = lax.broadcasted_iota(jnp.int32, (q, QUAD_DIM), 1) % q
    causal = row >= col
    diag = row == col
    brow = lax.broadcasted_iota(jnp.int32, (QUAD_DIM, QUAD_DIM), 0) // q
    bcol = lax.broadcasted_iota(jnp.int32, (QUAD_DIM, QUAD_DIM), 1) // HEAD_DIM
    blockdiag = brow == bcol

    def chunk(c, carry):
        rows = pl.ds(pl.multiple_of(c * q, q), q)
        dt, da = _dt_and_decay(dt_ref[0, rows, :], dtb_ref, alog_ref)
        cum = _cumsum_rows(tri3_ref[...], da)
        last = cum[q - 1:q, :]
        p_cum, p_dt = _pack3(cum), _pack3(dt)
        p_w, p_ecum = _pack3(jnp.exp(last - cum)), _pack3(jnp.exp(cum))

        for g in range(GROUPS):
            gl = slice(g * GROUP_DIM, (g + 1) * GROUP_DIM)
            ex = exp3_ref[:, gl]
            cum_g, dt_g, w_g, ecum_g = (_dot(p, ex) for p in (p_cum, p_dt, p_w, p_ecum))
            xs = jnp.concatenate([act_ref[g * GROUP_SLABS + i, rows, :] for i in range(GROUP_SLABS)],
                                 axis=1)
            bm_g = act_ref[XS_SLABS + g, rows, :].astype(BF16)
            cm_g = act_ref[XS_SLABS + GROUPS + g, rows, :].astype(BF16)
            xdt = xs * dt_g
            xdt_b = xdt.astype(BF16)
            xw_b = (xdt * w_g).astype(BF16)

            scores = _dot_nt(cm_g, jnp.concatenate([bm_g] * QUAD, axis=0))
            y_parts = []
            for k in range(GROUP_DIM // QUAD_DIM):
                kl = slice(k * QUAD_DIM, (k + 1) * QUAD_DIM)
                cum_k = cum_g[:, kl]
                cum_row = jnp.sum(jnp.where(diag, cum_k, 0.0), axis=0, keepdims=True)
                decay = jnp.exp(jnp.where(causal, cum_k - cum_row, NEG_BIG))
                m = (scores * decay).astype(BF16)
                xk = jnp.concatenate([xdt_b[:, kl]] * QUAD, axis=0)
                xk = jnp.where(blockdiag, xk, jnp.zeros_like(xk))
                y_parts.append(_dot(m, xk))

            ht = ht_ref[:, gl]
            y_inter = _dot(cm_g, ht.astype(BF16))
            ht_ref[:, gl] = ht * ecum_g[q - 1:q, :] + _dot_tn(bm_g, xw_b)
            y = jnp.concatenate(y_parts, axis=1) + y_inter * ecum_g + dsk_ref[:, gl] * xs
            y_ref[0, rows, gl] = _gated_norm_group(y, z_ref[0, rows, gl], nw_ref[:, gl]).astype(BF16)
        return carry

    lax.fori_loop(0, tq // q, chunk, 0, unroll=True)

    @pl.when(t == pl.num_programs(1) - 1)
    def _():
        hnew_ref[0] = ht_ref[...].T


def _ssd_prompt(proj, prefix, h0t, lw, layer, consts, *, tq):
    b, l, _ = proj.shape
    full = lambda *shape: pl.BlockSpec(shape, lambda i, t: (0,) * len(shape))
    return pl.pallas_call(
        _ssd_prompt_kernel,
        grid=(b, l // tq),
        in_specs=[pl.BlockSpec((1, tq, CONV_DIM), lambda i, t: (i, t, COL_XBC // CONV_DIM)),
                  pl.BlockSpec((1, tq, D_INNER), lambda i, t: (i, t, COL_Z // D_INNER)),
                  pl.BlockSpec((1, tq, LANES), lambda i, t: (i, t, COL_DT // LANES)),
                  full(1, CONV_KEEP, CONV_DIM),
                  full(1, D_STATE, D_INNER),
                  _whole(layer, lw["conv_w_slab"]), _whole(layer, lw["conv_b_slab"]),
                  _whole(layer, lw["dt_bias"]), _whole(layer, lw["a_log"]),
                  _whole(layer, lw["d_skip"]), _whole(layer, lw["ssd_norm_w"]),
                  full(LANES, D_INNER), full(*consts["tri3_prompt"].shape)],
        out_specs=[pl.BlockSpec((1, tq, D_INNER), lambda i, t: (i, t, 0)),
                   pl.BlockSpec((1, CONV_KEEP, CONV_DIM), lambda i, t: (i, 0, 0)),
                   pl.BlockSpec((1, D_INNER, D_STATE), lambda i, t: (i, 0, 0))],
        out_shape=[jax.ShapeDtypeStruct((b, l, D_INNER), BF16),
                   jax.ShapeDtypeStruct((b, CONV_KEEP, CONV_DIM), F32),
                   jax.ShapeDtypeStruct((b, D_INNER, D_STATE), F32)],
        scratch_shapes=[pltpu.VMEM((CONV_SLABS, CONV_PAD + tq, LANES), F32),
                        pltpu.VMEM((CONV_SLABS, tq, LANES), F32),
                        pltpu.VMEM((D_STATE, D_INNER), F32)],
        compiler_params=_params("parallel", "arbitrary"),
        name="ssd_prompt",
    )(proj, proj, proj, prefix, h0t, lw["conv_w_slab"], lw["conv_b_slab"], lw["dt_bias"], lw["a_log"],
      lw["d_skip"], lw["ssd_norm_w"], consts["expand3"], consts["tri3_prompt"])


def _conv_silu(ext_ref, row0, n, cw_ref, cb_ref):
    acc = cb_ref[...]
    for k in range(CONV_WIDTH):
        r = row0 - CONV_KEEP + k
        acc = acc + ext_ref[r:r + n, :] * cw_ref[k:k + 1, :]
    return _silu(acc)


def _ssd_short_kernel(xbc_ref, z_ref, dt_ref, pre_ref, h0_ref, cw_ref, cb_ref, dtb_ref, alog_ref,
                      dsk_ref, nw_ref, exp3_ref, tri3_ref, *rest, emit_transposed, aliased):
    rest = rest[1:] if aliased else rest
    y_ref, cnew_ref, hnew_ref = rest[:3]
    if emit_transposed:
        hnewt_ref, ext_ref, cume_ref, dte_ref = rest[3:]
    else:
        ext_ref, cume_ref, dte_ref = rest[3:]
    bb, q, _ = xbc_ref.shape

    dt, da = _dt_and_decay(dt_ref[...].reshape(bb * q, LANES), dtb_ref, alog_ref)
    cume_ref[...] = _expand_heads(_cumsum_rows(tri3_ref[...], da), exp3_ref[...])
    dte_ref[...] = _expand_heads(dt, exp3_ref[...])
    trow = lax.broadcasted_iota(jnp.int32, (q, D_INNER), 0)

    def one(i, carry):
        ext_ref[CONV_PAD - CONV_KEEP:CONV_PAD, :] = pre_ref[i]
        ext_ref[CONV_PAD:CONV_PAD + q, :] = xbc_ref[i]
        cnew_ref[i] = ext_ref[CONV_PAD + q - CONV_KEEP:CONV_PAD + q, :]
        act = _conv_silu(ext_ref, CONV_PAD, q, cw_ref, cb_ref)
        xs = act[:, :D_INNER]
        bm = act[:, D_INNER:D_INNER + GROUPS * D_STATE]
        cm = act[:, D_INNER + GROUPS * D_STATE:]
        r0 = pl.multiple_of(i * q, q)
        cum_e = cume_ref[pl.ds(r0, q), :]
        dt_e = dte_ref[pl.ds(r0, q), :]
        last_e = cum_e[q - 1:q, :]
        xdt = xs * dt_e
        xw_b = (xdt * jnp.exp(last_e - cum_e)).astype(BF16)
        ht = h0_ref[i].T
        ht_b = ht.astype(BF16)

        y = dsk_ref[...] * xs
        for s in range(q):
            decay = jnp.exp(jnp.where(trow >= s, cum_e - cum_e[s:s + 1, :], NEG_BIG))
            coef = []
            for g in range(GROUPS):
                sl = slice(g * D_STATE, (g + 1) * D_STATE)
                sc = jnp.sum(cm[:, sl] * bm[s:s + 1, sl], axis=-1, keepdims=True)
                coef.append(jnp.broadcast_to(sc, (q, GROUP_DIM)))
            y = y + jnp.concatenate(coef, axis=1) * decay * xdt[s:s + 1, :]

        bm_b, cm_b = bm.astype(BF16), cm.astype(BF16)
        y_inter, h_add = [], []
        for g in range(GROUPS):
            sl = slice(g * D_STATE, (g + 1) * D_STATE)
            gl = slice(g * GROUP_DIM, (g + 1) * GROUP_DIM)
            y_inter.append(_dot(cm_b[:, sl], ht_b[:, gl]))
            h_add.append(_dot_tn(bm_b[:, sl], xw_b[:, gl]))
        y = y + jnp.concatenate(y_inter, axis=1) * jnp.exp(cum_e)
        ht_new = ht * jnp.exp(last_e) + jnp.concatenate(h_add, axis=1)
        hnew_ref[i] = ht_new.T
        if emit_transposed:
            hnewt_ref[i] = ht_new
        z = z_ref[i]
        outs = [_gated_norm_group(y[:, g * GROUP_DIM:(g + 1) * GROUP_DIM],
                                  z[:, g * GROUP_DIM:(g + 1) * GROUP_DIM],
                                  nw_ref[:, g * GROUP_DIM:(g + 1) * GROUP_DIM]) for g in range(GROUPS)]
        y_ref[i] = jnp.concatenate(outs, axis=1).astype(BF16)
        return carry

    lax.fori_loop(0, bb, one, 0)


def _ssd_short(proj, prefix, pre_layer, h0, h0_layer, h_prev, lw, layer, consts, tri3, *,
               bb, out_depth, out_layer, emit_transposed):
    b, q, _ = proj.shape
    full = lambda *shape: pl.BlockSpec(shape, lambda i: (0,) * len(shape))
    aliased = h_prev is not None
    in_specs = [pl.BlockSpec((bb, q, CONV_DIM), lambda i: (i, 0, COL_XBC // CONV_DIM)),
                pl.BlockSpec((bb, q, D_INNER), lambda i: (i, 0, COL_Z // D_INNER)),
                pl.BlockSpec((bb, q, LANES), lambda i: (i, 0, COL_DT // LANES)),
                _layer_spec(pre_layer, (bb, CONV_KEEP, CONV_DIM), lambda i: (i, 0, 0)),
                _layer_spec(h0_layer, (bb, D_INNER, D_STATE), lambda i: (i, 0, 0)),
                _whole(layer, lw["conv_w"]), _whole(layer, lw["conv_b"]),
                _whole(layer, lw["dt_bias"]), _whole(layer, lw["a_log"]),
                _whole(layer, lw["d_skip"]), _whole(layer, lw["ssd_norm_w"]),
                full(LANES, D_INNER), full(*tri3.shape)]
    args = [proj, proj, proj, prefix, h0, lw["conv_w"], lw["conv_b"], lw["dt_bias"], lw["a_log"],
            lw["d_skip"], lw["ssd_norm_w"], consts["expand3"], tri3]
    if aliased:
        in_specs.append(pl.BlockSpec(memory_space=pl.ANY))
        args.append(h_prev)
    out_specs = [pl.BlockSpec((bb, q, D_INNER), lambda i: (i, 0, 0)),
                 pl.BlockSpec((bb, CONV_KEEP, CONV_DIM), lambda i: (i, 0, 0)),
                 _layer_spec(out_layer, (bb, D_INNER, D_STATE), lambda i: (i, 0, 0))]
    out_shape = [jax.ShapeDtypeStruct((b, q, D_INNER), BF16),
                 jax.ShapeDtypeStruct((b, CONV_KEEP, CONV_DIM), F32),
                 jax.ShapeDtypeStruct((out_depth, b, D_INNER, D_STATE), F32)]
    if emit_transposed:
        out_specs.append(pl.BlockSpec((bb, D_STATE, D_INNER), lambda i: (i, 0, 0)))
        out_shape.append(jax.ShapeDtypeStruct((b, D_STATE, D_INNER), F32))
    return pl.pallas_call(
        functools.partial(_ssd_short_kernel, emit_transposed=emit_transposed, aliased=aliased),
        grid=(b // bb,),
        in_specs=in_specs,
        out_specs=out_specs,
        out_shape=out_shape,
        input_output_aliases={len(args) - 1: 2} if aliased else {},
        scratch_shapes=[pltpu.VMEM((CONV_PAD + q, CONV_DIM), F32),
                        pltpu.VMEM((bb * q, D_INNER), F32),
                        pltpu.VMEM((bb * q, D_INNER), F32)],
        compiler_params=_params("parallel"),
        name="ssd_short",
    )(*args)


def _outproj_kernel(y_ref, a_ref, gb_ref, x_ref, ws_ref, wo_ref, o_ref):
    b_out = _dot(y_ref[...], ws_ref[...])
    merged = a_ref[...] + jax.nn.sigmoid(gb_ref[...]) * b_out
    o_ref[...] = x_ref[...] + _dot(merged.astype(BF16), wo_ref[...])


def _outproj(y2d, a2d, proj2d, x2d, lw, layer):
    rows = x2d.shape[0]
    tm = min(rows, 1024)
    return pl.pallas_call(
        _outproj_kernel,
        grid=(rows // tm,),
        in_specs=[pl.BlockSpec((tm, D_INNER), lambda i: (i, 0)),
                  pl.BlockSpec((tm, D_MODEL), lambda i: (i, 0)),
                  pl.BlockSpec((tm, D_MODEL), lambda i: (i, COL_GB // D_MODEL)),
                  pl.BlockSpec((tm, D_MODEL), lambda i: (i, 0)),
                  _whole(layer, lw["w_ssd_out"]), _whole(layer, lw["w_o"])],
        out_specs=pl.BlockSpec((tm, D_MODEL), lambda i: (i, 0)),
        out_shape=jax.ShapeDtypeStruct((rows, D_MODEL), F32),
        compiler_params=_params("parallel"),
        name="outproj",
    )(y2d, a2d, proj2d, x2d, lw["w_ssd_out"], lw["w_o"])


FFN_PAD = SUBLANES


def _ffn_kernel(x_ref, pre_ref, nw_ref, wu_ref, cw_ref, cb_ref, wd_ref, o_ref, new_ref, ext_ref):
    bb, lt, _ = x_ref.shape
    top = FFN_PAD + lt
    t = pl.program_id(1)

    @pl.when(t == 0)
    def _():
        ext_ref[:, FFN_PAD - FFN_KEEP:FFN_PAD, :] = pre_ref[...]

    x = x_ref[...].reshape(bb * lt, D_MODEL)
    hn = _rmsnorm(x, nw_ref[...]).astype(BF16)
    ext_ref[:, FFN_PAD:top, :] = _dot(hn, wu_ref[...]).reshape(bb, lt, 2 * D_FF)
    tail = ext_ref[:, top - FFN_KEEP:top, :]

    acc_g = cb_ref[:, :D_FF]
    acc_v = cb_ref[:, D_FF:]
    for k in range(FFN_CONV_WIDTH):
        r = FFN_PAD - FFN_KEEP + k
        acc_g = acc_g + ext_ref[:, r:r + lt, :D_FF] * cw_ref[k:k + 1, :D_FF]
        acc_v = acc_v + ext_ref[:, r:r + lt, D_FF:] * cw_ref[k:k + 1, D_FF:]
    act = (_silu(acc_g) * acc_v).reshape(bb * lt, D_FF).astype(BF16)
    o_ref[...] = (x + _dot(act, wd_ref[...])).reshape(bb, lt, D_MODEL)

    ext_ref[:, FFN_PAD - FFN_KEEP:FFN_PAD, :] = tail

    @pl.when(t == pl.num_programs(1) - 1)
    def _():
        new_ref[...] = tail


def _ffn(x, prefix, pre_layer, lw, layer, *, bb, lt):
    b, l, _ = x.shape
    pb = prefix.shape[1]
    pre_spec = _layer_spec(pre_layer, (bb if pb == b else 1, FFN_KEEP, 2 * D_FF),
                           lambda i, t: (i if pb == b else 0, 0, 0))
    return pl.pallas_call(
        _ffn_kernel,
        grid=(b // bb, l // lt),
        in_specs=[pl.BlockSpec((bb, lt, D_MODEL), lambda i, t: (i, t, 0)),
                  pre_spec,
                  _whole(layer, lw["norm2_w"]), _whole(layer, lw["w_up"]),
                  _whole(layer, lw["ffn_conv_w"]), _whole(layer, lw["ffn_conv_b"]),
                  _whole(layer, lw["w_down"])],
        out_specs=[pl.BlockSpec((bb, lt, D_MODEL), lambda i, t: (i, t, 0)),
                   pl.BlockSpec((bb, FFN_KEEP, 2 * D_FF), lambda i, t: (i, 0, 0))],
        out_shape=[jax.ShapeDtypeStruct((b, l, D_MODEL), F32),
                   jax.ShapeDtypeStruct((b, FFN_KEEP, 2 * D_FF), F32)],
        scratch_shapes=[pltpu.VMEM((bb, FFN_PAD + lt, 2 * D_FF), F32)],
        compiler_params=_params("parallel", "arbitrary"),
        name="conv_ffn",
    )(x, prefix, lw["norm2_w"], lw["w_up"], lw["ffn_conv_w"], lw["ffn_conv_b"], lw["w_down"])


def _final_norm_kernel(x_ref, w_ref, o_ref):
    o_ref[...] = _rmsnorm(x_ref[...], w_ref[...])


def _final_norm(x2d, w):
    rows = x2d.shape[0]
    tm = min(rows, 1024)
    return pl.pallas_call(
        _final_norm_kernel,
        grid=(rows // tm,),
        in_specs=[pl.BlockSpec((tm, D_MODEL), lambda i: (i, 0)),
                  pl.BlockSpec((1, D_MODEL), lambda i: (0, 0))],
        out_specs=pl.BlockSpec((tm, D_MODEL), lambda i: (i, 0)),
        out_shape=jax.ShapeDtypeStruct((rows, D_MODEL), F32),
        compiler_params=_params("parallel"),
        name="final_norm",
    )(x2d, w)


STREAMS = {
    "meta": dict(pool=(1, N_META), ffn=(1, N_META), ssd_bb=1, start_pos=0),
    "prompt": dict(pool=(1, 512), ffn=(1, 256), ssd_tq=256, start_pos=N_META),
    "sample": dict(pool=(32, 8), ffn=(16, 8), ssd_bb=8, start_pos=PAST_LEN),
}


def _causal_blocks3(rows, q):
    i = jnp.arange(rows)[:, None]
    j = jnp.arange(rows)[None, :]
    m = ((i // q == j // q) & (j <= i)).astype(BF16)
    m3 = jnp.concatenate([m, m, m], axis=1)
    return jnp.pad(m3, ((0, 0), (0, -m3.shape[1] % LANES)))


def _constants():
    r = jnp.arange(LANES)[:, None]
    c = jnp.arange(D_INNER)[None, :]
    expand3 = ((r < 3 * HEADS) & (r % HEADS == c // HEAD_DIM)).astype(BF16)
    return dict(
        expand3=expand3,
        tri3_prompt=_causal_blocks3(SSD_CHUNK, SSD_CHUNK),
        tri3_meta=_causal_blocks3(N_META, N_META),
        tri3_sample=_causal_blocks3(STREAMS["sample"]["ssd_bb"] * 8, 8),
    )


def _dense_stages(x, proj, a_gated, y, ffn_pre, ffn_pre_layer, lw, layer, cfg):
    b, l, _ = x.shape
    x1 = _outproj(y.reshape(b * l, D_INNER), a_gated.reshape(b * l, D_MODEL),
                  proj.reshape(b * l, D_PROJ), x.reshape(b * l, D_MODEL), lw, layer).reshape(b, l, D_MODEL)
    return _ffn(x1, ffn_pre, ffn_pre_layer, lw, layer, bb=cfg["ffn"][0], lt=cfg["ffn"][1])


def kernel(x_prompt, x_sample, state_pool, state_conv, state_ssm, state_ffn, meta_tokens, norm1_w, w_in, pool_w, pool_scale, w_pool_out, conv_w, conv_b, dt_bias, a_log, d_skip, ssd_norm_w, w_ssd_out, w_o, norm2_w, w_up, ffn_conv_w, ffn_conv_b, w_down, final_norm_w):
    b_p = x_prompt.shape[0]
    b_s = x_sample.shape[0]

    w_in_r = jnp.concatenate([
        w_in[..., 3072:6144], w_in[..., 0:1024], w_in[..., 1024:3072],
        w_in[..., 6176:7200], w_in[..., 7200:8224], w_in[..., 6144:6176],
        jnp.zeros((DEPTH, D_MODEL, LANES - HEADS), F32)], axis=-1).astype(BF16)
    pad_heads = lambda v: jnp.pad(v, ((0, 0), (0, LANES - HEADS)))[:, None, :]
    lw = dict(
        norm1_w=norm1_w[:, None, :], w_in=w_in_r, pool_w=pool_w.astype(BF16),
        pool_scale=pool_scale[:, None, :], w_pool_out=w_pool_out.astype(BF16),
        conv_w=conv_w, conv_b=conv_b[:, None, :],
        conv_w_slab=conv_w.reshape(DEPTH, CONV_WIDTH, CONV_SLABS, LANES).transpose(0, 2, 1, 3),
        conv_b_slab=conv_b.reshape(DEPTH, CONV_SLABS, 1, LANES),
        dt_bias=pad_heads(dt_bias), a_log=pad_heads(a_log),
        d_skip=jnp.repeat(d_skip, HEAD_DIM, axis=-1)[:, None, :], ssd_norm_w=ssd_norm_w[:, None, :],
        w_ssd_out=w_ssd_out.astype(BF16), w_o=w_o.astype(BF16), norm2_w=norm2_w[:, None, :],
        w_up=w_up.astype(BF16), ffn_conv_w=ffn_conv_w, ffn_conv_b=ffn_conv_b[:, None, :],
        w_down=w_down.astype(BF16))
    consts = _constants()
    cm, cp, cs = STREAMS["meta"], STREAMS["prompt"], STREAMS["sample"]

    xm = meta_tokens.astype(x_prompt.dtype)[None]
    xp = x_prompt
    xs = x_sample
    zeros = lambda *shape: jnp.zeros((1, 1) + shape, F32)
    zero_pool, zero_conv = zeros(POOL_BUF, D_POOL), zeros(CONV_KEEP, CONV_DIM)
    zero_ssm, zero_ffn = zeros(D_INNER, D_STATE), zeros(FFN_KEEP, 2 * D_FF)
    ssm_in = state_ssm.reshape(DEPTH, b_s, D_INNER, D_STATE)
    s_ssm = None
    p_states, s_states = [], []
    for l in range(DEPTH):
        proj = _inproj(xm.reshape(N_META, D_MODEL), lw, l).reshape(1, N_META, D_PROJ)
        a_g, m_pool = _pool_branch(proj, zero_pool, 0, lw, l, bb=1, lt=N_META, start_pos=cm["start_pos"])
        y, m_conv, _, m_ssm_t = _ssd_short(proj, zero_conv, 0, zero_ssm, 0, None, lw, l, consts,
                                           consts["tri3_meta"], bb=1, out_depth=1, out_layer=0,
                                           emit_transposed=True)
        xm, m_ffn = _dense_stages(xm, proj, a_g, y, zero_ffn, 0, lw, l, cm)

        proj = _inproj(xp.reshape(-1, D_MODEL), lw, l).reshape(b_p, -1, D_PROJ)
        a_g, p_pool = _pool_branch(proj, m_pool[None], 0, lw, l, bb=cp["pool"][0], lt=cp["pool"][1],
                                   start_pos=cp["start_pos"])
        y, p_conv, p_ssm = _ssd_prompt(proj, m_conv, m_ssm_t, lw, l, consts, tq=cp["ssd_tq"])
        xp, p_ffn = _dense_stages(xp, proj, a_g, y, m_ffn[None], 0, lw, l, cp)
        p_states.append((p_pool, p_conv, p_ssm, p_ffn))

        proj = _inproj(xs.reshape(-1, D_MODEL), lw, l).reshape(b_s, -1, D_PROJ)
        a_g, s_pool = _pool_branch(proj, state_pool, l, lw, l, bb=cs["pool"][0], lt=cs["pool"][1],
                                   start_pos=cs["start_pos"])
        y, s_conv, s_ssm = _ssd_short(proj, state_conv, l, ssm_in, l, s_ssm, lw, l, consts,
                                      consts["tri3_sample"], bb=cs["ssd_bb"], out_depth=DEPTH,
                                      out_layer=l, emit_transposed=False)
        xs, s_ffn = _dense_stages(xs, proj, a_g, y, state_ffn, l, lw, l, cs)
        s_states.append((s_pool, s_conv, s_ffn))

    fw = final_norm_w[None, :]
    y_prompt = _final_norm(xp.reshape(-1, D_MODEL), fw).reshape(xp.shape)
    y_sample = _final_norm(xs.reshape(-1, D_MODEL), fw).reshape(xs.shape)

    def stack(states, idx):
        return jnp.stack([s[idx] for s in states])

    return (y_prompt, y_sample,
            stack(p_states, 0), stack(p_states, 1),
            stack(p_states, 2).reshape(DEPTH, b_p, HEADS, HEAD_DIM, D_STATE), stack(p_states, 3),
            stack(s_states, 0), stack(s_states, 1),
            s_ssm.reshape(DEPTH, b_s, HEADS, HEAD_DIM, D_STATE), stack(s_states, 2))
```

```python
import functools

import jax
import jax.numpy as jnp
from jax import lax
from jax.experimental import pallas as pl
from jax.experimental.pallas import tpu as pltpu

F32 = jnp.float32
BF16 = jnp.bfloat16

D_MODEL = 1024
DEPTH = 4
N_META = 16
D_POOL = D_MODEL
POOL_WINDOWS = (2, 4, 8, 16)
POOL_GROUP_DIM = D_POOL // len(POOL_WINDOWS)
POOL_BUF = max(POOL_WINDOWS) - 1
D_INNER = 2 * D_MODEL
HEAD_DIM = 64
HEADS = D_INNER // HEAD_DIM
GROUPS = 4
GROUP_DIM = D_INNER // GROUPS
D_STATE = 128
CONV_WIDTH = 4
CONV_KEEP = CONV_WIDTH - 1
CONV_DIM = D_INNER + 2 * GROUPS * D_STATE
D_FF = 2816
FFN_CONV_WIDTH = 3
FFN_KEEP = FFN_CONV_WIDTH - 1
EPS = 1e-6
PAST_LEN = 16384

LANES = 128
SUBLANES = 8
MXU_DIM = 256
VMEM_LIMIT_BYTES = 56 * 1024 * 1024

COL_XBC = 0
COL_U = COL_XBC + CONV_DIM
COL_Z = COL_U + D_POOL
COL_GA = COL_Z + D_INNER
COL_GB = COL_GA + D_MODEL
COL_DT = COL_GB + D_MODEL
D_PROJ = COL_DT + LANES

SSD_CHUNK = 64
QUAD = MXU_DIM // SSD_CHUNK
QUAD_DIM = QUAD * HEAD_DIM
CONV_SLABS = CONV_DIM // LANES
XS_SLABS = D_INNER // LANES
GROUP_SLABS = GROUP_DIM // LANES
CONV_STRIDE = 4
NEG_BIG = -1e30


def _params(*sem):
    return pltpu.CompilerParams(dimension_semantics=sem, vmem_limit_bytes=VMEM_LIMIT_BYTES)


def _layer_spec(layer, shape, index_map):
    return pl.BlockSpec((None,) + tuple(shape), lambda *g: (layer,) + tuple(index_map(*g)))


def _whole(layer, arr, single_buffer=False):
    shape = arr.shape[1:]
    if single_buffer:
        return pl.BlockSpec((None,) + tuple(shape), lambda *g: (layer,) + (0,) * len(shape),
                            pipeline_mode=pl.Buffered(1))
    return _layer_spec(layer, shape, lambda *g: (0,) * len(shape))


def _rmsnorm(x, w):
    ms = jnp.mean(x * x, axis=-1, keepdims=True)
    return x * lax.rsqrt(ms + EPS) * w


def _silu(x):
    return x * jax.nn.sigmoid(x)


def _dot(a, b):
    return jnp.dot(a, b, preferred_element_type=F32)


def _dot_nt(a, b):
    return lax.dot_general(a, b, (((1,), (1,)), ((), ())), preferred_element_type=F32)


def _dot_tn(a, b):
    return lax.dot_general(a, b, (((0,), (0,)), ((), ())), preferred_element_type=F32)


def _split3(v):
    hi = v.astype(BF16)
    r1 = v - hi.astype(F32)
    mid = r1.astype(BF16)
    lo = (r1 - mid.astype(F32)).astype(BF16)
    return hi, mid, lo


def _cumsum_rows(mask3, v):
    parts = list(_split3(v))
    pad = mask3.shape[1] - 3 * v.shape[0]
    if pad:
        parts.append(jnp.zeros((pad, v.shape[1]), BF16))
    return _dot(mask3, jnp.concatenate(parts, axis=0))


def _pack3(v):
    lane = lax.broadcasted_iota(jnp.int32, v.shape, 1)
    hi, mid, lo = _split3(jnp.where(lane < HEADS, v, 0.0))
    packed = (hi.astype(F32) + pltpu.roll(mid.astype(F32), HEADS, 1)
              + pltpu.roll(lo.astype(F32), 2 * HEADS, 1))
    return packed.astype(BF16)


def _expand_heads(v, expand3):
    return _dot(_pack3(v), expand3)


def _inproj_kernel(x_ref, nw_ref, w_ref, o_ref, hn_ref):
    @pl.when(pl.program_id(1) == 0)
    def _():
        hn_ref[...] = _rmsnorm(x_ref[...], nw_ref[...]).astype(BF16)

    o_ref[...] = _dot(hn_ref[...], w_ref[...])


def _inproj(x2d, lw, layer):
    rows = x2d.shape[0]
    tm = min(rows, 1024)
    tn = D_PROJ // 5
    return pl.pallas_call(
        _inproj_kernel,
        grid=(rows // tm, D_PROJ // tn),
        in_specs=[pl.BlockSpec((tm, D_MODEL), lambda i, j: (i, 0)),
                  _whole(layer, lw["norm1_w"]),
                  _layer_spec(layer, (D_MODEL, tn), lambda i, j: (0, j))],
        out_specs=pl.BlockSpec((tm, tn), lambda i, j: (i, j)),
        out_shape=jax.ShapeDtypeStruct((rows, D_PROJ), F32),
        scratch_shapes=[pltpu.VMEM((tm, D_MODEL), BF16)],
        compiler_params=_params("parallel", "arbitrary"),
        name="inproj",
    )(x2d, lw["norm1_w"], lw["w_in"])


POOL_PAD = 3 * SUBLANES


def _pool_kernel(u_ref, ga_ref, pre_ref, pw_ref, ps_ref, wo_ref, a_ref, new_ref,
                 ext_ref, lva_ref, lvb_ref, *, start_pos):
    bb, lt, _ = u_ref.shape
    top = POOL_PAD + lt
    t = pl.program_id(1)

    @pl.when(t == 0)
    def _():
        ext_ref[:, 0:POOL_PAD - POOL_BUF, :] = jnp.zeros((bb, POOL_PAD - POOL_BUF, D_POOL), F32)
        ext_ref[:, POOL_PAD - POOL_BUF:POOL_PAD, :] = pre_ref[...]
        lva_ref[:, 0:SUBLANES, :] = jnp.zeros((bb, SUBLANES, POOL_GROUP_DIM), F32)
        lvb_ref[:, 0:SUBLANES, :] = jnp.zeros((bb, SUBLANES, POOL_GROUP_DIM), F32)

    u = u_ref[...]
    ext_ref[:, POOL_PAD:top, :] = u

    if start_pos + 1 < max(POOL_WINDOWS):
        pos = start_pos + t * lt + lax.broadcasted_iota(jnp.int32, (bb, lt, POOL_GROUP_DIM), 1)

    mixed = []
    for g, win in enumerate(POOL_WINDOWS):
        c0, c1 = g * POOL_GROUP_DIM, (g + 1) * POOL_GROUP_DIM
        src, shift = ext_ref, 1
        bufs = [lva_ref, lvb_ref]
        cols = slice(c0, c1)
        while 2 * shift < win:
            dst = bufs[0]
            dst[:, SUBLANES:top, :] = (src[:, SUBLANES:top, cols]
                                       + src[:, SUBLANES - shift:top - shift, cols])
            src, cols, shift = dst, slice(0, POOL_GROUP_DIM), 2 * shift
            bufs = bufs[::-1]
        wsum = src[:, POOL_PAD:top, cols] + src[:, POOL_PAD - shift:top - shift, cols]
        ug = u[:, :, c0:c1]
        if start_pos + 1 >= max(POOL_WINDOWS):
            d = wsum * (1.0 / win) - ug
        else:
            d = wsum / jnp.minimum(pos + 1, win).astype(F32) - ug
        d = d.reshape(bb * lt, POOL_GROUP_DIM).astype(BF16)
        mixed.append((_dot(d, pw_ref[g]) * ps_ref[:, c0:c1]).astype(BF16))
    a_out = _dot(jnp.concatenate(mixed, axis=1), wo_ref[...])
    gate = jax.nn.sigmoid(ga_ref[...].reshape(bb * lt, D_MODEL))
    a_ref[...] = (gate * a_out).reshape(bb, lt, D_MODEL)

    tail = ext_ref[:, top - POOL_BUF:top, :]
    ext_ref[:, POOL_PAD - POOL_BUF:POOL_PAD, :] = tail

    @pl.when(t == pl.num_programs(1) - 1)
    def _():
        new_ref[...] = tail


def _pool_branch(proj, prefix, pre_layer, lw, layer, *, bb, lt, start_pos):
    b, l, _ = proj.shape
    pb = prefix.shape[1]
    pre_spec = _layer_spec(pre_layer, (bb if pb == b else 1, POOL_BUF, D_POOL),
                           lambda i, t: (i if pb == b else 0, 0, 0))
    return pl.pallas_call(
        functools.partial(_pool_kernel, start_pos=start_pos),
        grid=(b // bb, l // lt),
        in_specs=[pl.BlockSpec((bb, lt, D_POOL), lambda i, t: (i, t, COL_U // D_POOL)),
                  pl.BlockSpec((bb, lt, D_MODEL), lambda i, t: (i, t, COL_GA // D_MODEL)),
                  pre_spec,
                  _whole(layer, lw["pool_w"]), _whole(layer, lw["pool_scale"]),
                  _whole(layer, lw["w_pool_out"])],
        out_specs=[pl.BlockSpec((bb, lt, D_MODEL), lambda i, t: (i, t, 0)),
                   pl.BlockSpec((bb, POOL_BUF, D_POOL), lambda i, t: (i, 0, 0))],
        out_shape=[jax.ShapeDtypeStruct((b, l, D_MODEL), F32),
                   jax.ShapeDtypeStruct((b, POOL_BUF, D_POOL), F32)],
        scratch_shapes=[pltpu.VMEM((bb, POOL_PAD + lt, D_POOL), F32),
                        pltpu.VMEM((bb, POOL_PAD + lt, POOL_GROUP_DIM), F32),
                        pltpu.VMEM((bb, POOL_PAD + lt, POOL_GROUP_DIM), F32)],
        compiler_params=_params("parallel", "arbitrary"),
        name="pool_branch",
    )(proj, proj, prefix, lw["pool_w"], lw["pool_scale"], lw["w_pool_out"])


CONV_PAD = SUBLANES


def _dt_and_decay(dt_raw, dtb_ref, alog_ref):
    lane = lax.broadcasted_iota(jnp.int32, dt_raw.shape, 1)
    v = dt_raw + dtb_ref[...]
    dt = jnp.maximum(v, 0.0) + jnp.log1p(jnp.exp(-jnp.abs(v)))
    dt = jnp.where(lane < HEADS, dt, 0.0)
    return dt, dt * (-jnp.exp(alog_ref[...]))


def _gated_norm_group(y, z, w):
    yf = y * _silu(z)
    return yf * lax.rsqrt(jnp.mean(yf * yf, axis=-1, keepdims=True) + EPS) * w


def _ssd_prompt_kernel(xbc_ref, z_ref, dt_ref, pre_ref, h0_ref, cw_ref, cb_ref, dtb_ref, alog_ref,
                       dsk_ref, nw_ref, exp3_ref, tri3_ref, y_ref, cnew_ref, hnew_ref,
                       xslab_ref, act_ref, ht_ref):
    tq = xbc_ref.shape[1]
    q = SSD_CHUNK
    t = pl.program_id(1)
    tail_rows = slice(CONV_PAD + tq - CONV_KEEP, CONV_PAD + tq)
    lanes_of = lambda c: slice(c * LANES, (c + 1) * LANES)

    @pl.when(t == 0)
    def _():
        for c in range(CONV_SLABS):
            xslab_ref[c, CONV_PAD - CONV_KEEP:CONV_PAD, :] = pre_ref[0, :, lanes_of(c)]
        ht_ref[...] = h0_ref[0]

    for c in range(CONV_SLABS):
        xslab_ref[c, CONV_PAD:CONV_PAD + tq, :] = xbc_ref[0, :, lanes_of(c)]

    n_str = tq // CONV_STRIDE

    def conv_slab(c, carry):
        w = cw_ref[c]
        taps = {off: xslab_ref[c, pl.ds(CONV_PAD + off, n_str, stride=CONV_STRIDE), :]
                for off in range(-CONV_KEEP, CONV_STRIDE)}
        for j in range(CONV_STRIDE):
            acc = cb_ref[c]
            for k in range(CONV_WIDTH):
                acc = acc + taps[j - CONV_KEEP + k] * w[k:k + 1, :]
            act_ref[c, pl.ds(j, n_str, stride=CONV_STRIDE), :] = _silu(acc)
        return carry

    lax.fori_loop(0, CONV_SLABS, conv_slab, 0)

    @pl.when(t == pl.num_programs(1) - 1)
    def _():
        for c in range(CONV_SLABS):
            cnew_ref[0, :, lanes_of(c)] = xslab_ref[c, tail_rows, :]

    xslab_ref[:, CONV_PAD - CONV_KEEP:CONV_PAD, :] = xslab_ref[:, tail_rows, :]

    row = lax.broadcasted_iota(jnp.int32, (q, QUAD_DIM), 0)
    col = lax.broadcasted_iota(jnp.int32, (q, QUAD_DIM), 1) % q
    causal = row >= col
    diag = row == col
    brow = lax.broadcasted_iota(jnp.int32, (QUAD_DIM, QUAD_DIM), 0) // q
    bcol = lax.broadcasted_iota(jnp.int32, (QUAD_DIM, QUAD_DIM), 1) // HEAD_DIM
    blockdiag = brow == bcol

    def chunk(c, carry):
        rows = pl.ds(pl.multiple_of(c * q, q), q)
        dt, da = _dt_and_decay(dt_ref[0, rows, :], dtb_ref, alog_ref)
        cum = _cumsum_rows(tri3_ref[...], da)
        last = cum[q - 1:q, :]
        p_cum, p_dt = _pack3(cum), _pack3(dt)
        p_w, p_ecum = _pack3(jnp.exp(last - cum)), _pack3(jnp.exp(cum))

        for g in range(GROUPS):
            gl = slice(g * GROUP_DIM, (g + 1) * GROUP_DIM)
            ex = exp3_ref[:, gl]
            cum_g, dt_g, w_g, ecum_g = (_dot(p, ex) for p in (p_cum, p_dt, p_w, p_ecum))
            xs = jnp.concatenate([act_ref[g * GROUP_SLABS + i, rows, :] for i in range(GROUP_SLABS)],
                                 axis=1)
            bm_g = act_ref[XS_SLABS + g, rows, :].astype(BF16)
            cm_g = act_ref[XS_SLABS + GROUPS + g, rows, :].astype(BF16)
            xdt = xs * dt_g
            xdt_b = xdt.astype(BF16)
            xw_b = (xdt * w_g).astype(BF16)

            scores = _dot_nt(cm_g, jnp.concatenate([bm_g] * QUAD, axis=0))
            y_parts = []
            for k in range(GROUP_DIM // QUAD_DIM):
                kl = slice(k * QUAD_DIM, (k + 1) * QUAD_DIM)
                cum_k = cum_g[:, kl]
                cum_row = jnp.sum(jnp.where(diag, cum_k, 0.0), axis=0, keepdims=True)
                decay = jnp.exp(jnp.where(causal, cum_k - cum_row, NEG_BIG))
                m = (scores * decay).astype(BF16)
                xk = jnp.concatenate([xdt_b[:, kl]] * QUAD, axis=0)
                xk = jnp.where(blockdiag, xk, jnp.zeros_like(xk))
                y_parts.append(_dot(m, xk))

            ht = ht_ref[:, gl]
            y_inter = _dot(cm_g, ht.astype(BF16))
            ht_ref[:, gl] = ht * ecum_g[q - 1:q, :] + _dot_tn(bm_g, xw_b)
            y = jnp.concatenate(y_parts, axis=1) + y_inter * ecum_g + dsk_ref[:, gl] * xs
            y_ref[0, rows, gl] = _gated_norm_group(y, z_ref[0, rows, gl], nw_ref[:, gl]).astype(BF16)
        return carry

    lax.fori_loop(0, tq // q, chunk, 0, unroll=True)

    @pl.when(t == pl.num_programs(1) - 1)
    def _():
        hnew_ref[0] = ht_ref[...].T


def _ssd_prompt(proj, prefix, h0t, lw, layer, consts, *, tq):
    b, l, _ = proj.shape
    full = lambda *shape: pl.BlockSpec(shape, lambda i, t: (0,) * len(shape))
    return pl.pallas_call(
        _ssd_prompt_kernel,
        grid=(b, l // tq),
        in_specs=[pl.BlockSpec((1, tq, CONV_DIM), lambda i, t: (i, t, COL_XBC // CONV_DIM)),
                  pl.BlockSpec((1, tq, D_INNER), lambda i, t: (i, t, COL_Z // D_INNER)),
                  pl.BlockSpec((1, tq, LANES), lambda i, t: (i, t, COL_DT // LANES)),
                  full(1, CONV_KEEP, CONV_DIM),
                  full(1, D_STATE, D_INNER),
                  _whole(layer, lw["conv_w_slab"]), _whole(layer, lw["conv_b_slab"]),
                  _whole(layer, lw["dt_bias"]), _whole(layer, lw["a_log"]),
                  _whole(layer, lw["d_skip"]), _whole(layer, lw["ssd_norm_w"]),
                  full(LANES, D_INNER), full(*consts["tri3_prompt"].shape)],
        out_specs=[pl.BlockSpec((1, tq, D_INNER), lambda i, t: (i, t, 0)),
                   pl.BlockSpec((1, CONV_KEEP, CONV_DIM), lambda i, t: (i, 0, 0)),
                   pl.BlockSpec((1, D_INNER, D_STATE), lambda i, t: (i, 0, 0))],
        out_shape=[jax.ShapeDtypeStruct((b, l, D_INNER), BF16),
                   jax.ShapeDtypeStruct((b, CONV_KEEP, CONV_DIM), F32),
                   jax.ShapeDtypeStruct((b, D_INNER, D_STATE), F32)],
        scratch_shapes=[pltpu.VMEM((CONV_SLABS, CONV_PAD + tq, LANES), F32),
                        pltpu.VMEM((CONV_SLABS, tq, LANES), F32),
                        pltpu.VMEM((D_STATE, D_INNER), F32)],
        compiler_params=_params("parallel", "arbitrary"),
        name="ssd_prompt",
    )(proj, proj, proj, prefix, h0t, lw["conv_w_slab"], lw["conv_b_slab"], lw["dt_bias"], lw["a_log"],
      lw["d_skip"], lw["ssd_norm_w"], consts["expand3"], consts["tri3_prompt"])


def _conv_silu(ext_ref, row0, n, cw_ref, cb_ref):
    acc = cb_ref[...]
    for k in range(CONV_WIDTH):
        r = row0 - CONV_KEEP + k
        acc = acc + ext_ref[r:r + n, :] * cw_ref[k:k + 1, :]
    return _silu(acc)


def _ssd_short_kernel(xbc_ref, z_ref, dt_ref, pre_ref, h0_ref, cw_ref, cb_ref, dtb_ref, alog_ref,
                      dsk_ref, nw_ref, exp3_ref, tri3_ref, *rest, emit_transposed, aliased):
    rest = rest[1:] if aliased else rest
    y_ref, cnew_ref, hnew_ref = rest[:3]
    if emit_transposed:
        hnewt_ref, ext_ref, cume_ref, dte_ref = rest[3:]
    else:
        ext_ref, cume_ref, dte_ref = rest[3:]
    bb, q, _ = xbc_ref.shape

    dt, da = _dt_and_decay(dt_ref[...].reshape(bb * q, LANES), dtb_ref, alog_ref)
    cume_ref[...] = _expand_heads(_cumsum_rows(tri3_ref[...], da), exp3_ref[...])
    dte_ref[...] = _expand_heads(dt, exp3_ref[...])
    trow = lax.broadcasted_iota(jnp.int32, (q, D_INNER), 0)

    def one(i, carry):
        ext_ref[CONV_PAD - CONV_KEEP:CONV_PAD, :] = pre_ref[i]
        ext_ref[CONV_PAD:CONV_PAD + q, :] = xbc_ref[i]
        cnew_ref[i] = ext_ref[CONV_PAD + q - CONV_KEEP:CONV_PAD + q, :]
        act = _conv_silu(ext_ref, CONV_PAD, q, cw_ref, cb_ref)
        xs = act[:, :D_INNER]
        bm = act[:, D_INNER:D_INNER + GROUPS * D_STATE]
        cm = act[:, D_INNER + GROUPS * D_STATE:]
        r0 = pl.multiple_of(i * q, q)
        cum_e = cume_ref[pl.ds(r0, q), :]
        dt_e = dte_ref[pl.ds(r0, q), :]
        last_e = cum_e[q - 1:q, :]
        xdt = xs * dt_e
        xw_b = (xdt * jnp.exp(last_e - cum_e)).astype(BF16)
        ht = h0_ref[i].T
        ht_b = ht.astype(BF16)

        y = dsk_ref[...] * xs
        for s in range(q):
            decay = jnp.exp(jnp.where(trow >= s, cum_e - cum_e[s:s + 1, :], NEG_BIG))
            coef = []
            for g in range(GROUPS):
                sl = slice(g * D_STATE, (g + 1) * D_STATE)
                sc = jnp.sum(cm[:, sl] * bm[s:s + 1, sl], axis=-1, keepdims=True)
                coef.append(jnp.broadcast_to(sc, (q, GROUP_DIM)))
            y = y + jnp.concatenate(coef, axis=1) * decay * xdt[s:s + 1, :]

        bm_b, cm_b = bm.astype(BF16), cm.astype(BF16)
        y_inter, h_add = [], []
        for g in range(GROUPS):
            sl = slice(g * D_STATE, (g + 1) * D_STATE)
            gl = slice(g * GROUP_DIM, (g + 1) * GROUP_DIM)
            y_inter.append(_dot(cm_b[:, sl], ht_b[:, gl]))
            h_add.append(_dot_tn(bm_b[:, sl], xw_b[:, gl]))
        y = y + jnp.concatenate(y_inter, axis=1) * jnp.exp(cum_e)
        ht_new = ht * jnp.exp(last_e) + jnp.concatenate(h_add, axis=1)
        hnew_ref[i] = ht_new.T
        if emit_transposed:
            hnewt_ref[i] = ht_new
        z = z_ref[i]
        outs = [_gated_norm_group(y[:, g * GROUP_DIM:(g + 1) * GROUP_DIM],
                                  z[:, g * GROUP_DIM:(g + 1) * GROUP_DIM],
                                  nw_ref[:, g * GROUP_DIM:(g + 1) * GROUP_DIM]) for g in range(GROUPS)]
        y_ref[i] = jnp.concatenate(outs, axis=1).astype(BF16)
        return carry

    lax.fori_loop(0, bb, one, 0)


def _ssd_short(proj, prefix, pre_layer, h0, h0_layer, h_prev, lw, layer, consts, tri3, *,
               bb, out_depth, out_layer, emit_transposed):
    b, q, _ = proj.shape
    full = lambda *shape: pl.BlockSpec(shape, lambda i: (0,) * len(shape))
    aliased = h_prev is not None
    in_specs = [pl.BlockSpec((bb, q, CONV_DIM), lambda i: (i, 0, COL_XBC // CONV_DIM)),
                pl.BlockSpec((bb, q, D_INNER), lambda i: (i, 0, COL_Z // D_INNER)),
                pl.BlockSpec((bb, q, LANES), lambda i: (i, 0, COL_DT // LANES)),
                _layer_spec(pre_layer, (bb, CONV_KEEP, CONV_DIM), lambda i: (i, 0, 0)),
                _layer_spec(h0_layer, (bb, D_INNER, D_STATE), lambda i: (i, 0, 0)),
                _whole(layer, lw["conv_w"]), _whole(layer, lw["conv_b"]),
                _whole(layer, lw["dt_bias"]), _whole(layer, lw["a_log"]),
                _whole(layer, lw["d_skip"]), _whole(layer, lw["ssd_norm_w"]),
                full(LANES, D_INNER), full(*tri3.shape)]
    args = [proj, proj, proj, prefix, h0, lw["conv_w"], lw["conv_b"], lw["dt_bias"], lw["a_log"],
            lw["d_skip"], lw["ssd_norm_w"], consts["expand3"], tri3]
    if aliased:
        in_specs.append(pl.BlockSpec(memory_space=pl.ANY))
        args.append(h_prev)
    out_specs = [pl.BlockSpec((bb, q, D_INNER), lambda i: (i, 0, 0)),
                 pl.BlockSpec((bb, CONV_KEEP, CONV_DIM), lambda i: (i, 0, 0)),
                 _layer_spec(out_layer, (bb, D_INNER, D_STATE), lambda i: (i, 0, 0))]
    out_shape = [jax.ShapeDtypeStruct((b, q, D_INNER), BF16),
                 jax.ShapeDtypeStruct((b, CONV_KEEP, CONV_DIM), F32),
                 jax.ShapeDtypeStruct((out_depth, b, D_INNER, D_STATE), F32)]
    if emit_transposed:
        out_specs.append(pl.BlockSpec((bb, D_STATE, D_INNER), lambda i: (i, 0, 0)))
        out_shape.append(jax.ShapeDtypeStruct((b, D_STATE, D_INNER), F32))
    return pl.pallas_call(
        functools.partial(_ssd_short_kernel, emit_transposed=emit_transposed, aliased=aliased),
        grid=(b // bb,),
        in_specs=in_specs,
        out_specs=out_specs,
        out_shape=out_shape,
        input_output_aliases={len(args) - 1: 2} if aliased else {},
        scratch_shapes=[pltpu.VMEM((CONV_PAD + q, CONV_DIM), F32),
                        pltpu.VMEM((bb * q, D_INNER), F32),
                        pltpu.VMEM((bb * q, D_INNER), F32)],
        compiler_params=_params("parallel"),
        name="ssd_short",
    )(*args)


def _outproj_kernel(y_ref, a_ref, gb_ref, x_ref, ws_ref, wo_ref, o_ref):
    b_out = _dot(y_ref[...], ws_ref[...])
    merged = a_ref[...] + jax.nn.sigmoid(gb_ref[...]) * b_out
    o_ref[...] = x_ref[...] + _dot(merged.astype(BF16), wo_ref[...])


def _outproj(y2d, a2d, proj2d, x2d, lw, layer):
    rows = x2d.shape[0]
    tm = min(rows, 1024)
    return pl.pallas_call(
        _outproj_kernel,
        grid=(rows // tm,),
        in_specs=[pl.BlockSpec((tm, D_INNER), lambda i: (i, 0)),
                  pl.BlockSpec((tm, D_MODEL), lambda i: (i, 0)),
                  pl.BlockSpec((tm, D_MODEL), lambda i: (i, COL_GB // D_MODEL)),
                  pl.BlockSpec((tm, D_MODEL), lambda i: (i, 0)),
                  _whole(layer, lw["w_ssd_out"]), _whole(layer, lw["w_o"])],
        out_specs=pl.BlockSpec((tm, D_MODEL), lambda i: (i, 0)),
        out_shape=jax.ShapeDtypeStruct((rows, D_MODEL), F32),
        compiler_params=_params("parallel"),
        name="outproj",
    )(y2d, a2d, proj2d, x2d, lw["w_ssd_out"], lw["w_o"])


FFN_PAD = SUBLANES


def _ffn_kernel(x_ref, pre_ref, nw_ref, wu_ref, cw_ref, cb_ref, wd_ref, o_ref, new_ref, ext_ref):
    bb, lt, _ = x_ref.shape
    top = FFN_PAD + lt
    t = pl.program_id(1)

    @pl.when(t == 0)
    def _():
        ext_ref[:, FFN_PAD - FFN_KEEP:FFN_PAD, :] = pre_ref[...]

    x = x_ref[...].reshape(bb * lt, D_MODEL)
    hn = _rmsnorm(x, nw_ref[...]).astype(BF16)
    ext_ref[:, FFN_PAD:top, :] = _dot(hn, wu_ref[...]).reshape(bb, lt, 2 * D_FF)
    tail = ext_ref[:, top - FFN_KEEP:top, :]

    acc_g = cb_ref[:, :D_FF]
    acc_v = cb_ref[:, D_FF:]
    for k in range(FFN_CONV_WIDTH):
        r = FFN_PAD - FFN_KEEP + k
        acc_g = acc_g + ext_ref[:, r:r + lt, :D_FF] * cw_ref[k:k + 1, :D_FF]
        acc_v = acc_v + ext_ref[:, r:r + lt, D_FF:] * cw_ref[k:k + 1, D_FF:]
    act = (_silu(acc_g) * acc_v).reshape(bb * lt, D_FF).astype(BF16)
    o_ref[...] = (x + _dot(act, wd_ref[...])).reshape(bb, lt, D_MODEL)

    ext_ref[:, FFN_PAD - FFN_KEEP:FFN_PAD, :] = tail

    @pl.when(t == pl.num_programs(1) - 1)
    def _():
        new_ref[...] = tail


def _ffn(x, prefix, pre_layer, lw, layer, *, bb, lt):
    b, l, _ = x.shape
    pb = prefix.shape[1]
    pre_spec = _layer_spec(pre_layer, (bb if pb == b else 1, FFN_KEEP, 2 * D_FF),
                           lambda i, t: (i if pb == b else 0, 0, 0))
    return pl.pallas_call(
        _ffn_kernel,
        grid=(b // bb, l // lt),
        in_specs=[pl.BlockSpec((bb, lt, D_MODEL), lambda i, t: (i, t, 0)),
                  pre_spec,
                  _whole(layer, lw["norm2_w"]), _whole(layer, lw["w_up"], single_buffer=True),
                  _whole(layer, lw["ffn_conv_w"]), _whole(layer, lw["ffn_conv_b"]),
                  _whole(layer, lw["w_down"], single_buffer=True)],
        out_specs=[pl.BlockSpec((bb, lt, D_MODEL), lambda i, t: (i, t, 0)),
                   pl.BlockSpec((bb, FFN_KEEP, 2 * D_FF), lambda i, t: (i, 0, 0))],
        out_shape=[jax.ShapeDtypeStruct((b, l, D_MODEL), F32),
                   jax.ShapeDtypeStruct((b, FFN_KEEP, 2 * D_FF), F32)],
        scratch_shapes=[pltpu.VMEM((bb, FFN_PAD + lt, 2 * D_FF), F32)],
        compiler_params=_params("parallel", "arbitrary"),
        name="conv_ffn",
    )(x, prefix, lw["norm2_w"], lw["w_up"], lw["ffn_conv_w"], lw["ffn_conv_b"], lw["w_down"])


def _final_norm_kernel(x_ref, w_ref, o_ref):
    o_ref[...] = _rmsnorm(x_ref[...], w_ref[...])


def _final_norm(x2d, w):
    rows = x2d.shape[0]
    tm = min(rows, 1024)
    return pl.pallas_call(
        _final_norm_kernel,
        grid=(rows // tm,),
        in_specs=[pl.BlockSpec((tm, D_MODEL), lambda i: (i, 0)),
                  pl.BlockSpec((1, D_MODEL), lambda i: (0, 0))],
        out_specs=pl.BlockSpec((tm, D_MODEL), lambda i: (i, 0)),
        out_shape=jax.ShapeDtypeStruct((rows, D_MODEL), F32),
        compiler_params=_params("parallel"),
        name="final_norm",
    )(x2d, w)


STREAMS = {
    "meta": dict(pool=(1, N_META), ffn=(1, N_META), ssd_bb=1, start_pos=0),
    "prompt": dict(pool=(1, 512), ffn=(1, 512), ssd_tq=256, start_pos=N_META),
    "sample": dict(pool=(32, 8), ffn=(16, 8), ssd_bb=8, start_pos=PAST_LEN),
}


def _causal_blocks3(rows, q):
    i = jnp.arange(rows)[:, None]
    j = jnp.arange(rows)[None, :]
    m = ((i // q == j // q) & (j <= i)).astype(BF16)
    m3 = jnp.concatenate([m, m, m], axis=1)
    return jnp.pad(m3, ((0, 0), (0, -m3.shape[1] % LANES)))


def _constants():
    r = jnp.arange(LANES)[:, None]
    c = jnp.arange(D_INNER)[None, :]
    expand3 = ((r < 3 * HEADS) & (r % HEADS == c // HEAD_DIM)).astype(BF16)
    return dict(
        expand3=expand3,
        tri3_prompt=_causal_blocks3(SSD_CHUNK, SSD_CHUNK),
        tri3_meta=_causal_blocks3(N_META, N_META),
        tri3_sample=_causal_blocks3(STREAMS["sample"]["ssd_bb"] * 8, 8),
    )


def _dense_stages(x, proj, a_gated, y, ffn_pre, ffn_pre_layer, lw, layer, cfg):
    b, l, _ = x.shape
    x1 = _outproj(y.reshape(b * l, D_INNER), a_gated.reshape(b * l, D_MODEL),
                  proj.reshape(b * l, D_PROJ), x.reshape(b * l, D_MODEL), lw, layer).reshape(b, l, D_MODEL)
    return _ffn(x1, ffn_pre, ffn_pre_layer, lw, layer, bb=cfg["ffn"][0], lt=cfg["ffn"][1])


def kernel(x_prompt, x_sample, state_pool, state_conv, state_ssm, state_ffn, meta_tokens, norm1_w, w_in, pool_w, pool_scale, w_pool_out, conv_w, conv_b, dt_bias, a_log, d_skip, ssd_norm_w, w_ssd_out, w_o, norm2_w, w_up, ffn_conv_w, ffn_conv_b, w_down, final_norm_w):
    b_p = x_prompt.shape[0]
    b_s = x_sample.shape[0]

    w_in_r = jnp.concatenate([
        w_in[..., 3072:6144], w_in[..., 0:1024], w_in[..., 1024:3072],
        w_in[..., 6176:7200], w_in[..., 7200:8224], w_in[..., 6144:6176],
        jnp.zeros((DEPTH, D_MODEL, LANES - HEADS), F32)], axis=-1).astype(BF16)
    pad_heads = lambda v: jnp.pad(v, ((0, 0), (0, LANES - HEADS)))[:, None, :]
    lw = dict(
        norm1_w=norm1_w[:, None, :], w_in=w_in_r, pool_w=pool_w.astype(BF16),
        pool_scale=pool_scale[:, None, :], w_pool_out=w_pool_out.astype(BF16),
        conv_w=conv_w, conv_b=conv_b[:, None, :],
        conv_w_slab=conv_w.reshape(DEPTH, CONV_WIDTH, CONV_SLABS, LANES).transpose(0, 2, 1, 3),
        conv_b_slab=conv_b.reshape(DEPTH, CONV_SLABS, 1, LANES),
        dt_bias=pad_heads(dt_bias), a_log=pad_heads(a_log),
        d_skip=jnp.repeat(d_skip, HEAD_DIM, axis=-1)[:, None, :], ssd_norm_w=ssd_norm_w[:, None, :],
        w_ssd_out=w_ssd_out.astype(BF16), w_o=w_o.astype(BF16), norm2_w=norm2_w[:, None, :],
        w_up=w_up.astype(BF16), ffn_conv_w=ffn_conv_w, ffn_conv_b=ffn_conv_b[:, None, :],
        w_down=w_down.astype(BF16))
    consts = _constants()
    cm, cp, cs = STREAMS["meta"], STREAMS["prompt"], STREAMS["sample"]

    xm = meta_tokens.astype(x_prompt.dtype)[None]
    xp = x_prompt
    xs = x_sample
    zeros = lambda *shape: jnp.zeros((1, 1) + shape, F32)
    zero_pool, zero_conv = zeros(POOL_BUF, D_POOL), zeros(CONV_KEEP, CONV_DIM)
    zero_ssm, zero_ffn = zeros(D_INNER, D_STATE), zeros(FFN_KEEP, 2 * D_FF)
    ssm_in = state_ssm.reshape(DEPTH, b_s, D_INNER, D_STATE)
    s_ssm = None
    p_states, s_states = [], []
    for l in range(DEPTH):
        proj = _inproj(xm.reshape(N_META, D_MODEL), lw, l).reshape(1, N_META, D_PROJ)
        a_g, m_pool = _pool_branch(proj, zero_pool, 0, lw, l, bb=1, lt=N_META, start_pos=cm["start_pos"])
        y, m_conv, _, m_ssm_t = _ssd_short(proj, zero_conv, 0, zero_ssm, 0, None, lw, l, consts,
                                           consts["tri3_meta"], bb=1, out_depth=1, out_layer=0,
                                           emit_transposed=True)
        xm, m_ffn = _dense_stages(xm, proj, a_g, y, zero_ffn, 0, lw, l, cm)

        proj = _inproj(xp.reshape(-1, D_MODEL), lw, l).reshape(b_p, -1, D_PROJ)
        a_g, p_pool = _pool_branch(proj, m_pool[None], 0, lw, l, bb=cp["pool"][0], lt=cp["pool"][1],
                                   start_pos=cp["start_pos"])
        y, p_conv, p_ssm = _ssd_prompt(proj, m_conv, m_ssm_t, lw, l, consts, tq=cp["ssd_tq"])
        xp, p_ffn = _dense_stages(xp, proj, a_g, y, m_ffn[None], 0, lw, l, cp)
        p_states.append((p_pool, p_conv, p_ssm, p_ffn))

        proj = _inproj(xs.reshape(-1, D_MODEL), lw, l).reshape(b_s, -1, D_PROJ)
        a_g, s_pool = _pool_branch(proj, state_pool, l, lw, l, bb=cs["pool"][0], lt=cs["pool"][1],
                                   start_pos=cs["start_pos"])
        y, s_conv, s_ssm = _ssd_short(proj, state_conv, l, ssm_in, l, s_ssm, lw, l, consts,
                                      consts["tri3_sample"], bb=cs["ssd_bb"], out_depth=DEPTH,
                                      out_layer=l, emit_transposed=False)
        xs, s_ffn = _dense_stages(xs, proj, a_g, y, state_ffn, l, lw, l, cs)
        s_states.append((s_pool, s_conv, s_ffn))

    fw = final_norm_w[None, :]
    y_prompt = _final_norm(xp.reshape(-1, D_MODEL), fw).reshape(xp.shape)
    y_sample = _final_norm(xs.reshape(-1, D_MODEL), fw).reshape(xs.shape)

    def stack(states, idx):
        return jnp.stack([s[idx] for s in states])

    return (y_prompt, y_sample,
            stack(p_states, 0), stack(p_states, 1),
            stack(p_states, 2).reshape(DEPTH, b_p, HEADS, HEAD_DIM, D_STATE), stack(p_states, 3),
            stack(s_states, 0), stack(s_states, 1),
            s_ssm.reshape(DEPTH, b_s, HEADS, HEAD_DIM, D_STATE), stack(s_states, 2))
```
